```python
import jax, jax.numpy as jnp
from jax import lax
import numpy as np

D_MODEL = 1024
BATCH = 4
SEQ = 8192
DEPTH = 2

N_MIXERS = 2
N_META = 16
FOX_HEADS = 16
FOX_HEAD_DIM = D_MODEL // FOX_HEADS
FOX_Q_BLOCK = 128
HGRN_EXPAND = 128
HGRN_HEADS = D_MODEL // HGRN_EXPAND
HGRN_DK = HGRN_EXPAND
HGRN_DV = D_MODEL // HGRN_HEADS
HGRN_CHUNK = 64
FFN_HIDDEN = -(-8 * D_MODEL // (3 * 256)) * 256
N_FOX = (DEPTH + N_MIXERS - 1) // N_MIXERS
N_HGRN = DEPTH // N_MIXERS
EPS = 1e-6

kernel_name = "fox_hgrn2_interleaved_meta_trunk"


def rms_norm(x, gain):
    xf = x.astype(jnp.float32)
    y = xf * lax.rsqrt(jnp.mean(xf * xf, axis=-1, keepdims=True) + EPS)
    return (y * gain.astype(jnp.float32)).astype(x.dtype)


def _fox_attend(q_blk, c_q, pos_q, k, v, c_k, pos_k):
    s = jnp.einsum('bhqd,bhkd->bhqk', q_blk, k).astype(jnp.float32) * (FOX_HEAD_DIM ** -0.5)
    s = s + (c_q[..., :, None] - c_k[..., None, :])
    s = jnp.where(pos_k[None, :] <= pos_q[:, None], s, -jnp.inf)
    p = jax.nn.softmax(s, axis=-1)
    return jnp.einsum('bhqk,bhkd->bhqd', p, v.astype(jnp.float32))


def fox_mixer(h, w_in, b_f, q_gain, k_gain, w_out):
    B, L, D = h.shape
    n_blk = (L - N_META) // FOX_Q_BLOCK
    proj = h @ w_in
    q, k, v, gate, f_logit = jnp.split(proj, [D, 2 * D, 3 * D, 4 * D], axis=-1)
    q = rms_norm(q.reshape(B, L, FOX_HEADS, FOX_HEAD_DIM), q_gain)
    k = rms_norm(k.reshape(B, L, FOX_HEADS, FOX_HEAD_DIM), k_gain)
    v = v.reshape(B, L, FOX_HEADS, FOX_HEAD_DIM)
    log_f = jax.nn.log_sigmoid(f_logit.astype(jnp.float32) + b_f.astype(jnp.float32))
    c = jnp.cumsum(log_f, axis=1).transpose(0, 2, 1)
    q, k, v = (t.transpose(0, 2, 1, 3) for t in (q, k, v))
    pos = jnp.arange(L)
    o_meta = _fox_attend(q[:, :, :N_META], c[:, :, :N_META], pos[:N_META],
                         k[:, :, :N_META], v[:, :, :N_META], c[:, :, :N_META], pos[:N_META])
    qb = q[:, :, N_META:].reshape(B, FOX_HEADS, n_blk, FOX_Q_BLOCK, FOX_HEAD_DIM).transpose(2, 0, 1, 3, 4)
    cb = c[:, :, N_META:].reshape(B, FOX_HEADS, n_blk, FOX_Q_BLOCK).transpose(2, 0, 1, 3)
    pb = pos[N_META:].reshape(n_blk, FOX_Q_BLOCK)
    o_real = lax.map(lambda a: _fox_attend(a[0], a[1], a[2], k, v, c, pos), (qb, cb, pb))
    o_real = o_real.transpose(1, 2, 0, 3, 4).reshape(B, FOX_HEADS, L - N_META, FOX_HEAD_DIM)
    o = jnp.concatenate([o_meta, o_real], axis=2).transpose(0, 2, 1, 3).reshape(B, L, D)
    o = o * jax.nn.sigmoid(gate.astype(jnp.float32))
    return (o.astype(h.dtype) @ w_out).astype(h.dtype)


def _hgrn_chunk(S, inp):
    q, k, v, g = inp
    C = q.shape[2]
    b = jnp.cumsum(g, axis=2)
    causal = jnp.tril(jnp.ones((C, C), dtype=bool))
    diff = b[:, :, :, None, :] - b[:, :, None, :, :]
    decay = jnp.exp(jnp.where(causal[..., None], diff, -jnp.inf))
    attn = jnp.einsum('bhtd,bhsd,bhtsd->bhts', q, k, decay)
    o = jnp.einsum('bhts,bhsv->bhtv', attn, v) + jnp.einsum('bhtd,bhdv->bhtv', q * jnp.exp(b), S)
    b_last = b[:, :, -1:, :]
    S_new = jnp.exp(b_last[:, :, 0, :])[..., None] * S + jnp.einsum(
        'bhsd,bhsv->bhdv', k * jnp.exp(b_last - b), v)
    return S_new, o


def hgrn2_mixer(h, w_in, lb, g_gain, w_out):
    B, L, D = h.shape
    n_chunk = (L - N_META) // HGRN_CHUNK
    proj = h @ w_in
    q, f_logit, i, g_out = jnp.split(proj, 4, axis=-1)
    z = f_logit.astype(jnp.float32)
    lbf = lb.astype(jnp.float32)
    log_f = jnp.logaddexp(jnp.log(lbf), jnp.log1p(-lbf) + jax.nn.log_sigmoid(z))
    k = (1.0 - lbf) * jax.nn.sigmoid(-z)
    q = jax.nn.silu(q.astype(jnp.float32))
    v = i.astype(jnp.float32)
    heads = lambda t, d: t.reshape(B, L, HGRN_HEADS, d).transpose(0, 2, 1, 3)
    q, k, log_f, v = heads(q, HGRN_DK), heads(k, HGRN_DK), heads(log_f, HGRN_DK), heads(v, HGRN_DV)
    S0 = jnp.zeros((B, HGRN_HEADS, HGRN_DK, HGRN_DV), jnp.float32)
    S_meta, o_meta = _hgrn_chunk(S0, (q[:, :, :N_META], k[:, :, :N_META], v[:, :, :N_META], log_f[:, :, :N_META]))
    chunks = lambda t: t[:, :, N_META:].reshape(B, HGRN_HEADS, n_chunk, HGRN_CHUNK, t.shape[-1]).transpose(2, 0, 1, 3, 4)
    _, o_real = lax.scan(_hgrn_chunk, S_meta, (chunks(q), chunks(k), chunks(v), chunks(log_f)))
    o_real = o_real.transpose(1, 2, 0, 3, 4).reshape(B, HGRN_HEADS, L - N_META, HGRN_DV)
    o = jnp.concatenate([o_meta, o_real], axis=2).transpose(0, 2, 1, 3)
    o = rms_norm(o, g_gain) * jax.nn.silu(g_out.astype(jnp.float32).reshape(B, L, HGRN_HEADS, HGRN_DV))
    return (o.reshape(B, L, D).astype(h.dtype) @ w_out).astype(h.dtype)


def swiglu(h, w_in, w_out):
    gate, up = jnp.split(h @ w_in, 2, axis=-1)
    return ((jax.nn.silu(gate) * up) @ w_out).astype(h.dtype)


def setup_inputs(seed: int = 0) -> dict:
    key = jax.random.key(seed)
    ks = jax.random.split(key, 18)
    D = D_MODEL
    nrm = lambda k, shape, fan: jax.random.normal(k, shape, jnp.float32) * fan ** -0.5
    gain = lambda k, shape: 1.0 + 0.02 * jax.random.normal(k, shape, jnp.float32)
    return {
        "x": jax.random.normal(ks[0], (BATCH, SEQ, D), jnp.float32),
        "meta_tokens": jax.random.normal(ks[1], (N_META, D), jnp.float32),
        "attn_norm": gain(ks[2], (DEPTH, D)),
        "ffn_norm": gain(ks[3], (DEPTH, D)),
        "final_norm": gain(ks[4], (D,)),
        "fox_w_in": nrm(ks[5], (N_FOX, D, 4 * D + FOX_HEADS), D),
        "fox_b_f": jax.random.uniform(ks[6], (N_FOX, FOX_HEADS), jnp.float32, 1.0, 4.0),
        "fox_q_norm": gain(ks[7], (N_FOX, FOX_HEAD_DIM)),
        "fox_k_norm": gain(ks[8], (N_FOX, FOX_HEAD_DIM)),
        "fox_w_out": nrm(ks[9], (N_FOX, D, D), D),
        "hgrn_w_in": nrm(ks[10], (N_HGRN, D, 4 * D), D),
        "hgrn_lower_bounds": 0.1 * jax.random.normal(ks[11], (DEPTH, D), jnp.float32),
        "hgrn_g_norm": gain(ks[12], (N_HGRN, HGRN_DV)),
        "hgrn_w_out": nrm(ks[13], (N_HGRN, D, D), D),
        "ffn_w_in": nrm(ks[14], (DEPTH, D, 2 * FFN_HIDDEN), D),
        "ffn_w_out": nrm(ks[15], (DEPTH, FFN_HIDDEN, D), FFN_HIDDEN),
    }


def reference(x, meta_tokens, attn_norm, ffn_norm, final_norm, fox_w_in, fox_b_f, fox_q_norm,
              fox_k_norm, fox_w_out, hgrn_w_in, hgrn_lower_bounds, hgrn_g_norm, hgrn_w_out,
              ffn_w_in, ffn_w_out):
    B = x.shape[0]
    meta = jnp.broadcast_to(meta_tokens[None].astype(x.dtype), (B, N_META, D_MODEL))
    h = jnp.concatenate([meta, x], axis=1)
    lb_soft = jax.nn.softmax(hgrn_lower_bounds.astype(jnp.float32), axis=0)
    lower_bounds = jnp.cumsum(lb_soft, axis=0) - lb_soft[0]
    for i in range(DEPTH):
        hn = rms_norm(h, attn_norm[i])
        j = i // N_MIXERS
        if i % N_MIXERS == 0:
            h = h + fox_mixer(hn, fox_w_in[j], fox_b_f[j], fox_q_norm[j], fox_k_norm[j], fox_w_out[j])
        else:
            h = h + hgrn2_mixer(hn, hgrn_w_in[j], lower_bounds[i], hgrn_g_norm[j], hgrn_w_out[j])
        h = h + swiglu(rms_norm(h, ffn_norm[i]), ffn_w_in[i], ffn_w_out[i])
    h = rms_norm(h, final_norm)
    return h[:, N_META:]
```

```python
import functools

import jax
import jax.numpy as jnp
from jax import lax
from jax.experimental import pallas as pl
from jax.experimental.pallas import tpu as pltpu

D_MODEL = 1024
N_META = 16
FOX_HEADS = 16
FOX_HEAD_DIM = 64
HGRN_HEADS = 8
HGRN_DK = 128
FFN_HIDDEN = 2816
EPS = 1e-6

LANES = 128
FOX_PAIRS = FOX_HEADS // 2
FFN_TILE = 256
NEG_BIG = -1e30
FAST_RANGE = 40.0
VMEM_LIMIT = 48 * 1024 * 1024

F32 = jnp.float32
BF16 = jnp.bfloat16
HIGHEST = lax.Precision.HIGHEST

NN = (((1,), (0,)), ((), ()))
NT = (((1,), (1,)), ((), ()))
TN = (((0,), (0,)), ((), ()))


def _dot(a, b, dims=NN, prec=None):
    return lax.dot_general(a, b, dims, precision=prec, preferred_element_type=F32)


def _params(*sem):
    return pltpu.CompilerParams(dimension_semantics=sem, vmem_limit_bytes=VMEM_LIMIT)


def _row_tile(m, pref):
    return pref if m % pref == 0 else m


def _sigmoid_pair(z):
    t = jnp.exp(-jnp.abs(z))
    r = 1.0 / (1.0 + t)
    small = t * r
    pos = z >= 0
    return jnp.where(pos, r, small), jnp.where(pos, small, r)


def _rms_rows(x, gain):
    ms = jnp.mean(x * x, axis=-1, keepdims=True)
    return x * lax.rsqrt(ms + EPS) * gain


def _fox_proj_kernel(x_ref, gain_ref, w_ref, wf_ref, bd_ref, ex_ref, qg_ref, kg_ref,
                     q_out, k_out, v_out, g_out, f_out, xn_s):
    j = pl.program_id(1)

    @pl.when(j == 0)
    def _():
        xn = _rms_rows(x_ref[...], gain_ref[...]).astype(BF16)
        xn_s[...] = xn
        f_out[...] = _dot(xn, wf_ref[...])

    acc = _dot(xn_s[...], w_ref[...])

    def head_norm(gain):
        ss = _dot((acc * acc).astype(BF16), bd_ref[...])
        r = lax.rsqrt(ss * (1.0 / FOX_HEAD_DIM) + EPS)
        r_hi = r.astype(BF16)
        r_lo = (r - r_hi.astype(F32)).astype(BF16)
        rr = _dot(r_hi, ex_ref[...]) + _dot(r_lo, ex_ref[...])
        return acc * rr * gain

    @pl.when(j == 0)
    def _():
        q_out[...] = head_norm(qg_ref[...]).astype(BF16)

    @pl.when(j == 1)
    def _():
        k_out[...] = head_norm(kg_ref[...]).astype(BF16)

    @pl.when(j == 2)
    def _():
        v_out[...] = acc.astype(BF16)

    @pl.when(j == 3)
    def _():
        g_out[...] = acc


def _fox_proj(x, gain, w, wf, bd, ex, qg, kg):
    m = x.shape[0]
    tm = _row_tile(m, 512)
    d = D_MODEL
    row = lambda i, j: (i, 0)
    const = lambda i, j: (0, 0)
    return pl.pallas_call(
        _fox_proj_kernel,
        grid=(m // tm, 4),
        in_specs=[
            pl.BlockSpec((tm, d), row),
            pl.BlockSpec((1, d), const),
            pl.BlockSpec((d, d), lambda i, j: (0, j)),
            pl.BlockSpec((d, LANES), const),
            pl.BlockSpec((d, LANES), const),
            pl.BlockSpec((LANES, d), const),
            pl.BlockSpec((1, d), const),
            pl.BlockSpec((1, d), const),
        ],
        out_specs=[
            pl.BlockSpec((tm, d), row),
            pl.BlockSpec((tm, d), row),
            pl.BlockSpec((tm, d), row),
            pl.BlockSpec((tm, d), row),
            pl.BlockSpec((tm, LANES), row),
        ],
        out_shape=[
            jax.ShapeDtypeStruct((m, d), BF16),
            jax.ShapeDtypeStruct((m, d), BF16),
            jax.ShapeDtypeStruct((m, d), BF16),
            jax.ShapeDtypeStruct((m, d), F32),
            jax.ShapeDtypeStruct((m, LANES), F32),
        ],
        scratch_shapes=[pltpu.VMEM((tm, d), BF16)],
        compiler_params=_params("arbitrary", "arbitrary"),
        name="fox_proj",
    )(x, gain, w, wf, bd, ex, qg, kg)


def _fox_cumsum_kernel(z_ref, bias_ref, c0_ref, tri_ref, c_out, carry_s):
    @pl.when(pl.program_id(1) == 0)
    def _():
        carry_s[...] = c0_ref[...]

    zz = z_ref[0] + bias_ref[...]
    log_f = jnp.minimum(zz, 0.0) - jnp.log1p(jnp.exp(-jnp.abs(zz)))
    cs = _dot(tri_ref[...], log_f, prec=HIGHEST) + carry_s[...]
    c_out[0] = cs
    tc = cs.shape[0]
    carry_s[...] = cs[tc - 1:tc, :]


def _fox_cumsum(z, bias, c0):
    b, l, _ = z.shape
    tc = _row_tile(l, 256)
    tri = jnp.tril(jnp.ones((tc, tc), F32))
    const = lambda i, t: (0, 0)
    return pl.pallas_call(
        _fox_cumsum_kernel,
        grid=(b, l // tc),
        in_specs=[
            pl.BlockSpec((1, tc, LANES), lambda i, t: (i, t, 0)),
            pl.BlockSpec((1, LANES), const),
            pl.BlockSpec((1, LANES), const),
            pl.BlockSpec((tc, tc), const),
        ],
        out_specs=pl.BlockSpec((1, tc, LANES), lambda i, t: (i, t, 0)),
        out_shape=jax.ShapeDtypeStruct((b, l, LANES), F32),
        scratch_shapes=[pltpu.VMEM((1, LANES), F32)],
        compiler_params=_params("arbitrary", "arbitrary"),
        name="fox_cumsum",
    )(z, bias, c0, tri)


def _fox_attn_kernel(has_meta, t, q_ref, k_ref, v_ref, c_ref, ct_ref, gate_ref, *rest):
    if has_meta:
        km_ref, vm_ref, cm_ref, o_ref, qa_s, cq_s, m_s, l_s, acc_s = rest
    else:
        o_ref, qa_s, cq_s, m_s, l_s, acc_s = rest
    hp = pl.program_id(1)
    i = pl.program_id(2)

    lane = lax.broadcasted_iota(jnp.int32, (1, LANES), 1)
    lo = lane < FOX_HEAD_DIM
    half = (lo, jnp.logical_not(lo))

    qt = q_ref[0]
    zero_b = jnp.zeros_like(qt)
    ct = c_ref[0]
    for a in range(2):
        qa_s[a] = jnp.where(half[a], qt, zero_b)
        cq_s[a] = jnp.sum(jnp.where(lane == 2 * hp + a, ct, 0.0), axis=1, keepdims=True)

    def online(a, s):
        m_prev = m_s[a]
        m_new = jnp.maximum(m_prev, jnp.max(s, axis=1, keepdims=True))
        alpha = jnp.exp(m_prev - m_new)
        p = jnp.exp(s - m_new)
        l_s[a] = alpha * l_s[a] + jnp.sum(p, axis=1, keepdims=True)
        m_s[a] = m_new
        return alpha, p.astype(BF16)

    def tile(kt, vt, ck, mask):
        alphas, pv = [], None
        zero_v = jnp.zeros_like(vt)
        for a in range(2):
            s = _dot(qa_s[a], kt, NT) + cq_s[a] - ck[a:a + 1, :]
            if mask is not None:
                s = jnp.where(mask, s, NEG_BIG)
            alpha, p = online(a, s)
            alphas.append(alpha)
            d = _dot(p, jnp.where(half[a], vt, zero_v))
            pv = d if pv is None else pv + d
        acc_s[...] = acc_s[...] * jnp.where(lo, alphas[0], alphas[1]) + pv

    m_s[...] = jnp.full(m_s.shape, NEG_BIG, F32)
    l_s[...] = jnp.zeros(l_s.shape, F32)
    acc_s[...] = jnp.zeros(acc_s.shape, F32)

    if has_meta:
        tile(km_ref[...], vm_ref[...], cm_ref[0], lane < N_META)

    def body(j, carry):
        off = pl.multiple_of(j * t, t)
        tile(k_ref[0, pl.ds(off, t), :], v_ref[0, pl.ds(off, t), :], ct_ref[0, 0, j], None)
        return carry

    lax.fori_loop(0, i, body, 0)

    off = pl.multiple_of(i * t, t)
    row = lax.broadcasted_iota(jnp.int32, (t, t), 0)
    col = lax.broadcasted_iota(jnp.int32, (t, t), 1)
    tile(k_ref[0, pl.ds(off, t), :], v_ref[0, pl.ds(off, t), :], ct_ref[0, 0, i], col <= row)

    inv = jnp.where(lo, 1.0 / l_s[0], 1.0 / l_s[1])
    sig, _ = _sigmoid_pair(gate_ref[0])
    o_ref[0] = (acc_s[...] * inv * sig).astype(BF16)


def _fox_attn(q, k, v, c, gate, meta=None):
    b, l, d = q.shape
    t = _row_tile(l, 512)
    nt = l // t
    ct = c[:, :, :FOX_HEADS].reshape(b, nt, t, FOX_PAIRS, 2).transpose(0, 3, 1, 4, 2)
    in_specs = [
        pl.BlockSpec((1, t, LANES), lambda bi, hp, i: (bi, i, hp)),
        pl.BlockSpec((1, l, LANES), lambda bi, hp, i: (bi, 0, hp)),
        pl.BlockSpec((1, l, LANES), lambda bi, hp, i: (bi, 0, hp)),
        pl.BlockSpec((1, t, LANES), lambda bi, hp, i: (bi, i, 0)),
        pl.BlockSpec((1, 1, nt, 2, t), lambda bi, hp, i: (bi, hp, 0, 0, 0)),
        pl.BlockSpec((1, t, LANES), lambda bi, hp, i: (bi, i, hp)),
    ]
    args = [q, k, v, c, ct, gate]
    if meta is not None:
        km, vm, cm = meta
        in_specs += [
            pl.BlockSpec((LANES, LANES), lambda bi, hp, i: (0, hp)),
            pl.BlockSpec((LANES, LANES), lambda bi, hp, i: (0, hp)),
            pl.BlockSpec((1, 2, LANES), lambda bi, hp, i: (hp, 0, 0)),
        ]
        args += [km, vm, cm]
    return pl.pallas_call(
        functools.partial(_fox_attn_kernel, meta is not None, t),
        grid=(b, FOX_PAIRS, nt),
        in_specs=in_specs,
        out_specs=pl.BlockSpec((1, t, LANES), lambda bi, hp, i: (bi, i, hp)),
        out_shape=jax.ShapeDtypeStruct((b, l, d), BF16),
        scratch_shapes=[
            pltpu.VMEM((2, t, LANES), BF16),
            pltpu.VMEM((2, t, 1), F32),
            pltpu.VMEM((2, t, 1), F32),
            pltpu.VMEM((2, t, 1), F32),
            pltpu.VMEM((t, LANES), F32),
        ],
        compiler_params=_params("arbitrary", "arbitrary", "arbitrary"),
        name="fox_attn",
    )(*args)


def _out_proj_kernel(y_ref, w_ref, h_ref, o_ref):
    o_ref[...] = h_ref[...] + _dot(y_ref[...], w_ref[...])


def _out_proj(y, w, h):
    m, d = h.shape
    tm = _row_tile(m, 512)
    row = lambda i: (i, 0)
    return pl.pallas_call(
        _out_proj_kernel,
        grid=(m // tm,),
        in_specs=[pl.BlockSpec((tm, d), row), pl.BlockSpec((d, d), lambda i: (0, 0)),
                  pl.BlockSpec((tm, d), row)],
        out_specs=pl.BlockSpec((tm, d), row),
        out_shape=jax.ShapeDtypeStruct((m, d), F32),
        compiler_params=_params("arbitrary"),
        name="out_proj",
    )(y, w, h)


def _ffn_kernel(final, h_ref, gain_ref, wg_ref, wu_ref, wo_ref, fgain_ref, o_ref, xn_s, acc_s):
    j = pl.program_id(1)

    @pl.when(j == 0)
    def _():
        xn_s[...] = _rms_rows(h_ref[...], gain_ref[...]).astype(BF16)
        acc_s[...] = jnp.zeros(acc_s.shape, F32)

    xn = xn_s[...]
    g = _dot(xn, wg_ref[...])
    u = _dot(xn, wu_ref[...])
    sig, _ = _sigmoid_pair(g)
    acc_s[...] += _dot((g * sig * u).astype(BF16), wo_ref[...])

    @pl.when(j == pl.num_programs(1) - 1)
    def _():
        out = h_ref[...] + acc_s[...]
        if final:
            out = _rms_rows(out, fgain_ref[...])
        o_ref[...] = out


def _ffn(h, gain, w_in, w_out, fgain, final):
    m, d = h.shape
    tm = _row_tile(m, 1024)
    nh = FFN_HIDDEN // FFN_TILE
    row = lambda i, j: (i, 0)
    const = lambda i, j: (0, 0)
    return pl.pallas_call(
        functools.partial(_ffn_kernel, final),
        grid=(m // tm, nh),
        in_specs=[
            pl.BlockSpec((tm, d), row),
            pl.BlockSpec((1, d), const),
            pl.BlockSpec((d, FFN_TILE), lambda i, j: (0, j)),
            pl.BlockSpec((d, FFN_TILE), lambda i, j: (0, j + nh)),
            pl.BlockSpec((FFN_TILE, d), lambda i, j: (j, 0)),
            pl.BlockSpec((1, d), const),
        ],
        out_specs=pl.BlockSpec((tm, d), row),
        out_shape=jax.ShapeDtypeStruct((m, d), F32),
        scratch_shapes=[pltpu.VMEM((tm, d), BF16), pltpu.VMEM((tm, d), F32)],
        compiler_params=_params("arbitrary", "arbitrary"),
        name="ffn",
    )(h, gain, w_in, w_in, w_out, fgain)


def _hgrn_proj_kernel(x_ref, gain_ref, w_ref, q_out, z_out, i_out, g_out, xn_s):
    j = pl.program_id(1)

    @pl.when(j == 0)
    def _():
        xn_s[...] = _rms_rows(x_ref[...], gain_ref[...]).astype(BF16)

    acc = _dot(xn_s[...], w_ref[...])

    @pl.when(j == 0)
    def _():
        q_out[...] = acc.astype(BF16)

    @pl.when(j == 1)
    def _():
        z_out[...] = acc

    @pl.when(j == 2)
    def _():
        i_out[...] = acc.astype(BF16)

    @pl.when(j == 3)
    def _():
        g_out[...] = acc.astype(BF16)


def _hgrn_proj(x, gain, w):
    m, d = x.shape
    tm = _row_tile(m, 512)
    row = lambda i, j: (i, 0)
    return pl.pallas_call(
        _hgrn_proj_kernel,
        grid=(m // tm, 4),
        in_specs=[pl.BlockSpec((tm, d), row), pl.BlockSpec((1, d), lambda i, j: (0, 0)),
                  pl.BlockSpec((d, d), lambda i, j: (0, j))],
        out_specs=[pl.BlockSpec((tm, d), row)] * 4,
        out_shape=[jax.ShapeDtypeStruct((m, d), BF16), jax.ShapeDtypeStruct((m, d), F32),
                   jax.ShapeDtypeStruct((m, d), BF16), jax.ShapeDtypeStruct((m, d), BF16)],
        scratch_shapes=[pltpu.VMEM((tm, d), BF16)],
        compiler_params=_params("arbitrary", "arbitrary"),
        name="hgrn_proj",
    )(x, gain, w)


def _hgrn_scan_kernel(c, q_ref, z_ref, i_ref, g_ref, lbp_ref, gg_ref, tri_ref, s0_ref,
                      y_ref, sfin_ref, st_s, b_s, k_s, v_s, q_s, o_s):
    step = pl.program_id(1)

    @pl.when(step == 0)
    def _():
        st_s[...] = s0_ref[...]

    lbp = lbp_ref[...]
    mx = jnp.maximum(lbp[0:1], lbp[1:2])
    e0 = jnp.exp(lbp[0:1] - mx)
    e1 = jnp.exp(lbp[1:2] - mx)
    den = e0 + e1
    sm0 = e0 / den
    lb = (sm0 + e1 / den) - sm0

    sig, sig_neg = _sigmoid_pair(z_ref[0])
    log_f = jnp.log(lb + (1.0 - lb) * sig)
    kk = (1.0 - lb) * sig_neg
    qz = q_ref[0].astype(F32)
    qq = qz * _sigmoid_pair(qz)[0]
    vv = i_ref[0].astype(F32)

    b = _dot(tri_ref[...], log_f, prec=HIGHEST)
    b_last = b[c - 1:c, :]
    mid = c // 2
    b_mid = b[mid - 1:mid, :]
    spread = jnp.maximum(jnp.max(b[0:1, :] - b_mid), jnp.max(b_mid - b_last))

    q_in = (qq * jnp.exp(b)).astype(BF16)
    k_out = (kk * jnp.exp(b_last - b)).astype(BF16)
    v_b = vv.astype(BF16)
    decay = jnp.exp(b_last)

    row = lax.broadcasted_iota(jnp.int32, (c, c), 0)
    col = lax.broadcasted_iota(jnp.int32, (c, c), 1)
    causal = col <= row

    @pl.when(spread <= FAST_RANGE)
    def _():
        qf = (qq * jnp.exp(b - b_mid)).astype(BF16)
        kf = (kk * jnp.exp(b_mid - b)).astype(BF16)
        for h in range(HGRN_HEADS):
            sl = slice(h * HGRN_DK, (h + 1) * HGRN_DK)
            a = jnp.where(causal, _dot(qf[:, sl], kf[:, sl], NT), 0.0)
            o_s[:, sl] = _dot(a.astype(BF16), v_b[:, sl])

    @pl.when(spread > FAST_RANGE)
    def _():
        b_s[...] = b
        k_s[...] = kk
        v_s[...] = vv
        q_s[...] = qq
        o_s[...] = jnp.zeros(o_s.shape, F32)
        rowc = lax.broadcasted_iota(jnp.int32, (c, 1), 0)

        def body(s, carry):
            bs = b_s[pl.ds(s, 1), :]
            p = q_s[...] * jnp.exp(jnp.minimum(b_s[...] - bs, 0.0)) * k_s[pl.ds(s, 1), :]
            vs = v_s[pl.ds(s, 1), :]
            seen = rowc >= s
            for h in range(HGRN_HEADS):
                sl = slice(h * HGRN_DK, (h + 1) * HGRN_DK)
                w = jnp.where(seen, jnp.sum(p[:, sl], axis=1, keepdims=True), 0.0)
                o_s[:, sl] += w * vs[:, sl]
            return carry

        lax.fori_loop(0, c, body, 0)

    gsig = _sigmoid_pair(g_ref[0].astype(F32))[0]
    gate = g_ref[0].astype(F32) * gsig
    for h in range(HGRN_HEADS):
        sl = slice(h * HGRN_DK, (h + 1) * HGRN_DK)
        st = st_s[h]
        o = o_s[:, sl] + _dot(q_in[:, sl], st.astype(BF16), NT)
        st_s[h] = st * decay[:, sl] + _dot(v_b[:, sl], k_out[:, sl], TN)
        y = _rms_rows(o, gg_ref[:, sl]) * gate[:, sl]
        y_ref[0, :, sl] = y.astype(BF16)

    @pl.when(step == pl.num_programs(1) - 1)
    def _():
        sfin_ref[...] = st_s[...]


def _hgrn_scan(q, z, i, g, lbp, gg, s0):
    b, l, d = q.shape
    c = _row_tile(l, 64)
    tri = jnp.tril(jnp.ones((c, c), F32))
    blk = lambda bi, t: (bi, t, 0)
    const2 = lambda bi, t: (0, 0)
    const3 = lambda bi, t: (0, 0, 0)
    st_shape = (HGRN_HEADS, HGRN_DK, HGRN_DK)
    return pl.pallas_call(
        functools.partial(_hgrn_scan_kernel, c),
        grid=(b, l // c),
        in_specs=[
            pl.BlockSpec((1, c, d), blk),
            pl.BlockSpec((1, c, d), blk),
            pl.BlockSpec((1, c, d), blk),
            pl.BlockSpec((1, c, d), blk),
            pl.BlockSpec((2, d), const2),
            pl.BlockSpec((1, d), const2),
            pl.BlockSpec((c, c), const2),
            pl.BlockSpec(st_shape, const3),
        ],
        out_specs=[pl.BlockSpec((1, c, d), blk), pl.BlockSpec(st_shape, const3)],
        out_shape=[jax.ShapeDtypeStruct((b, l, d), BF16), jax.ShapeDtypeStruct(st_shape, F32)],
        scratch_shapes=[pltpu.VMEM(st_shape, F32)] + [pltpu.VMEM((c, d), F32)] * 5,
        compiler_params=_params("arbitrary", "arbitrary"),
        name="hgrn_scan",
    )(q, z, i, g, lbp, gg, tri, s0)


def kernel(x, meta_tokens, attn_norm, ffn_norm, final_norm, fox_w_in, fox_b_f, fox_q_norm, fox_k_norm,
           fox_w_out, hgrn_w_in, hgrn_lower_bounds, hgrn_g_norm, hgrn_w_out, ffn_w_in, ffn_w_out):
    bsz, seq, d = x.shape
    row = lambda v: v.reshape(1, -1).astype(F32)

    w_fox = fox_w_in[0, :, :4 * d].astype(BF16)
    w_f = jnp.pad(fox_w_in[0, :, 4 * d:], ((0, 0), (0, LANES - FOX_HEADS))).astype(BF16)
    w_fox_out = fox_w_out[0].astype(BF16)
    w_hgrn = hgrn_w_in[0].astype(BF16)
    w_hgrn_out = hgrn_w_out[0].astype(BF16)
    w_ffn_in = ffn_w_in.astype(BF16)
    w_ffn_out = ffn_w_out.astype(BF16)
    head_of = jnp.arange(d) // FOX_HEAD_DIM
    bd = (head_of[:, None] == jnp.arange(LANES)[None, :]).astype(BF16)
    ex = bd.T
    qg = row(jnp.tile(fox_q_norm[0], FOX_HEADS)) * (FOX_HEAD_DIM ** -0.5)
    kg = row(jnp.tile(fox_k_norm[0], FOX_HEADS))
    b_f = jnp.pad(row(fox_b_f[0]), ((0, 0), (0, LANES - FOX_HEADS)))
    gg = row(jnp.tile(hgrn_g_norm[0], HGRN_HEADS))
    lbp = hgrn_lower_bounds.astype(F32)

    h_meta = meta_tokens.astype(F32)
    h_real = x.reshape(bsz * seq, d)

    proj = lambda h: _fox_proj(h, row(attn_norm[0]), w_fox, w_f, bd, ex, qg, kg)
    qm, km, vm, gm, fm = proj(h_meta)
    qr, kr, vr, gr, fr = proj(h_real)
    cm = _fox_cumsum(fm[None], b_f, jnp.zeros((1, LANES), F32))
    cr = _fox_cumsum(fr.reshape(bsz, seq, LANES), b_f, cm[0, N_META - 1:N_META])
    ym = _fox_attn(qm[None], km[None], vm[None], cm, gm[None])[0]
    pad_rows = lambda a: jnp.pad(a, ((0, LANES - N_META), (0, 0)))
    cm_keys = jnp.pad(cm[0, :, :FOX_HEADS].T, ((0, 0), (0, LANES - N_META))).reshape(FOX_PAIRS, 2, LANES)
    seq3 = lambda a: a.reshape(bsz, seq, -1)
    yr = _fox_attn(seq3(qr), seq3(kr), seq3(vr), cr, seq3(gr),
                   meta=(pad_rows(km), pad_rows(vm), cm_keys)).reshape(bsz * seq, d)
    h_meta = _out_proj(ym, w_fox_out, h_meta)
    h_real = _out_proj(yr, w_fox_out, h_real)
    ffn0 = lambda h: _ffn(h, row(ffn_norm[0]), w_ffn_in[0], w_ffn_out[0], row(final_norm), False)
    h_meta = ffn0(h_meta)
    h_real = ffn0(h_real)

    proj = lambda h: _hgrn_proj(h, row(attn_norm[1]), w_hgrn)
    qm, zm, im, gm = proj(h_meta)
    qr, zr, ir, gr = proj(h_real)
    s0 = jnp.zeros((HGRN_HEADS, HGRN_DK, HGRN_DK), F32)
    _, s_meta = _hgrn_scan(qm[None], zm[None], im[None], gm[None], lbp, gg, s0)
    yr, _ = _hgrn_scan(seq3(qr), seq3(zr), seq3(ir), seq3(gr), lbp, gg, s_meta)
    h_real = _out_proj(yr.reshape(bsz * seq, d), w_hgrn_out, h_real)
    out = _ffn(h_real, row(ffn_norm[1]), w_ffn_in[1], w_ffn_out[1], row(final_norm), True)
    return out.reshape(bsz, seq, d)
```

```python
import functools
import math

import jax
import jax.numpy as jnp
from jax import lax
from jax.experimental import pallas as pl
from jax.experimental.pallas import tpu as pltpu

D_MODEL = 1024
N_META = 16
FOX_HEADS = 16
FOX_HEAD_DIM = 64
HGRN_HEADS = 8
HGRN_DK = 128
FFN_HIDDEN = 2816
EPS = 1e-6

LANES = 128
FOX_PAIRS = FOX_HEADS // 2
FFN_TILE = 256
NEG_BIG = -1e30
PAD_KEY_BIAS = -30000.0
LOG2E = math.log2(math.e)
BOUND_LIMIT = 56.0
FAST_RANGE = 40.0
VMEM_LIMIT = 48 * 1024 * 1024

AUG_C = 0
AUG_ONE = 3
AUG_BOUND = 6
C3_MID, C3_LO, C3_ONE = 16, 32, 48

F32 = jnp.float32
BF16 = jnp.bfloat16
HIGHEST = lax.Precision.HIGHEST

NN = (((1,), (0,)), ((), ()))
NT = (((1,), (1,)), ((), ()))
TN = (((0,), (0,)), ((), ()))


def _dot(a, b, dims=NN, prec=None):
    return lax.dot_general(a, b, dims, precision=prec, preferred_element_type=F32)


def _params(*sem):
    return pltpu.CompilerParams(dimension_semantics=sem, vmem_limit_bytes=VMEM_LIMIT)


def _row_tile(m, pref):
    return pref if m % pref == 0 else m


def _sigmoid_pair(z):
    t = jnp.exp(-jnp.abs(z))
    r = 1.0 / (1.0 + t)
    small = t * r
    pos = z >= 0
    return jnp.where(pos, r, small), jnp.where(pos, small, r)


def _rms_rows(x, gain):
    ms = jnp.mean(x * x, axis=-1, keepdims=True)
    return x * lax.rsqrt(ms + EPS) * gain


def _fox_gate_kernel(x_ref, gain_ref, wf_ref, bias_ref, c0_ref, tri_ref, c_out, carry_s):
    @pl.when(pl.program_id(1) == 0)
    def _():
        carry_s[...] = c0_ref[...]

    xn = _rms_rows(x_ref[0], gain_ref[...]).astype(BF16)
    zz = _dot(xn, wf_ref[...]) + bias_ref[...]
    log_f = jnp.minimum(zz, 0.0) - jnp.log1p(jnp.exp(-jnp.abs(zz)))
    cs = _dot(tri_ref[...], log_f, prec=HIGHEST) + carry_s[...]
    c_out[0] = cs
    tc = cs.shape[0]
    carry_s[...] = cs[tc - 1:tc, :]


def _fox_gate(x, gain, wf, bias, c0):
    b, l, d = x.shape
    tc = _row_tile(l, 256)
    tri = jnp.tril(jnp.ones((tc, tc), F32))
    const = lambda i, t: (0, 0)
    return pl.pallas_call(
        _fox_gate_kernel,
        grid=(b, l // tc),
        in_specs=[
            pl.BlockSpec((1, tc, d), lambda i, t: (i, t, 0)),
            pl.BlockSpec((1, d), const),
            pl.BlockSpec((d, LANES), const),
            pl.BlockSpec((1, LANES), const),
            pl.BlockSpec((1, LANES), const),
            pl.BlockSpec((tc, tc), const),
        ],
        out_specs=pl.BlockSpec((1, tc, LANES), lambda i, t: (i, t, 0)),
        out_shape=jax.ShapeDtypeStruct((b, l, LANES), F32),
        scratch_shapes=[pltpu.VMEM((1, LANES), F32)],
        compiler_params=_params("arbitrary", "arbitrary"),
        name="fox_gate",
    )(x, gain, wf, bias, c0, tri)


def _fox_proj_kernel(x_ref, gain_ref, w_ref, c_ref, bd_ref, ex_ref, qg_ref, kg_ref, pq_ref, pk_ref,
                     q_out, k_out, v_out, g_out, xn_s):
    j = pl.program_id(1)

    @pl.when(j == 0)
    def _():
        xn_s[...] = _rms_rows(x_ref[...], gain_ref[...]).astype(BF16)

    acc = _dot(xn_s[...], w_ref[...])
    lane = lax.broadcasted_iota(jnp.int32, (1, LANES), 1)
    lo = lane < FOX_HEAD_DIM

    def head_slabs(gain, place_ref, out):
        ss = _dot((acc * acc).astype(BF16), bd_ref[...])
        r = lax.rsqrt(ss * (1.0 / FOX_HEAD_DIM) + EPS)
        r_hi = r.astype(BF16)
        r_lo = (r - r_hi.astype(F32)).astype(BF16)
        xh = acc * (_dot(r_hi, ex_ref[...]) + _dot(r_lo, ex_ref[...])) * gain
        cc = jnp.where(lane < FOX_HEADS, c_ref[...] * LOG2E, 0.0)
        hi = cc.astype(BF16).astype(F32)
        r1 = cc - hi
        mid = r1.astype(BF16).astype(F32)
        low = (r1 - mid).astype(BF16).astype(F32)
        c3 = (hi + pltpu.roll(mid, C3_MID, axis=1) + pltpu.roll(low, C3_LO, axis=1)
              + jnp.where(lane == C3_ONE, 1.0, 0.0))
        aug = _dot(c3.astype(BF16), place_ref[...])
        for h in range(FOX_HEADS):
            pair = slice((h // 2) * LANES, (h // 2 + 1) * LANES)
            keep = lo if h % 2 == 0 else jnp.logical_not(lo)
            out[h] = jnp.where(keep, xh[:, pair], aug[:, h * LANES:(h + 1) * LANES]).astype(BF16)

    @pl.when(j == 0)
    def _():
        head_slabs(qg_ref[...], pq_ref, q_out)

    @pl.when(j == 1)
    def _():
        head_slabs(kg_ref[...], pk_ref, k_out)

    @pl.when(j == 2)
    def _():
        v_out[...] = acc.astype(BF16)

    @pl.when(j == 3)
    def _():
        g_out[...] = acc


def _fox_proj(x, gain, w, c, bd, ex, qg, kg, pq, pk):
    m = x.shape[0]
    tm = _row_tile(m, 512)
    d = D_MODEL
    row = lambda i, j: (i, 0)
    const = lambda i, j: (0, 0)
    slab = pl.BlockSpec((FOX_HEADS, tm, LANES), lambda i, j: (0, i, 0))
    return pl.pallas_call(
        _fox_proj_kernel,
        grid=(m // tm, 4),
        in_specs=[
            pl.BlockSpec((tm, d), row),
            pl.BlockSpec((1, d), const),
            pl.BlockSpec((d, d), lambda i, j: (0, j)),
            pl.BlockSpec((tm, LANES), row),
            pl.BlockSpec((d, LANES), const),
            pl.BlockSpec((LANES, d), const),
            pl.BlockSpec((1, d), const),
            pl.BlockSpec((1, d), const),
            pl.BlockSpec((LANES, FOX_HEADS * LANES), const),
            pl.BlockSpec((LANES, FOX_HEADS * LANES), const),
        ],
        out_specs=[slab, slab, pl.BlockSpec((tm, d), row), pl.BlockSpec((tm, d), row)],
        out_shape=[
            jax.ShapeDtypeStruct((FOX_HEADS, m, LANES), BF16),
            jax.ShapeDtypeStruct((FOX_HEADS, m, LANES), BF16),
            jax.ShapeDtypeStruct((m, d), BF16),
            jax.ShapeDtypeStruct((m, d), F32),
        ],
        scratch_shapes=[pltpu.VMEM((tm, d), BF16)],
        compiler_params=_params("arbitrary", "arbitrary"),
        name="fox_proj",
    )(x, gain, w, c, bd, ex, qg, kg, pq, pk)


def _placement(bound2):
    pq = jnp.zeros((LANES, FOX_HEADS * LANES), F32)
    pk = jnp.zeros((LANES, FOX_HEADS * LANES), F32)
    for h in range(FOX_HEADS):
        base = h * LANES + (FOX_HEAD_DIM if h % 2 == 0 else 0)
        for piece, src in enumerate((h, C3_MID + h, C3_LO + h)):
            pq = pq.at[src, base + AUG_C + piece].set(1.0)
            pk = pk.at[src, base + AUG_ONE + piece].set(-1.0)
            pq = pq.at[C3_ONE, base + AUG_ONE + piece].set(1.0)
            pk = pk.at[C3_ONE, base + AUG_C + piece].set(1.0)
        pq = pq.at[C3_ONE, base + AUG_BOUND].set(1.0)
        pk = pk.at[C3_ONE, base + AUG_BOUND].set(-bound2)
    return pq.astype(BF16), pk.astype(BF16)


def _fox_attn_kernel(has_meta, t, lp, flag_ref, q_ref, k_ref, v_ref, gate_ref, *rest):
    if has_meta:
        km_ref, vm_ref, o_ref, m_s, l_s, lp_s, acc_s = rest
    else:
        o_ref, m_s, l_s, lp_s, acc_s = rest
    i = pl.program_id(2)

    lane = lax.broadcasted_iota(jnp.int32, (1, LANES), 1)
    lo = lane < FOX_HEAD_DIM
    half = (lo, jnp.logical_not(lo))

    def run(online):
        def tile(keys, vt, mask):
            alphas, pv = [], None
            zero_v = jnp.zeros_like(vt)
            for a in range(2):
                s = _dot(q_ref[a, 0], keys(a), NT)
                if mask is not None:
                    s = jnp.where(mask, s, NEG_BIG)
                if online:
                    m_prev = m_s[a]
                    m_new = jnp.maximum(m_prev, jnp.max(s, axis=1, keepdims=True))
                    alpha = jnp.exp2(m_prev - m_new)
                    p = jnp.exp2(s - m_new)
                    l_s[a] = alpha * l_s[a] + jnp.sum(p, axis=1, keepdims=True)
                    m_s[a] = m_new
                    alphas.append(alpha)
                else:
                    p = jnp.exp2(s)
                    part = lp_s[a]
                    for c0 in range(0, p.shape[1], lp):
                        part = part + p[:, c0:c0 + lp]
                    lp_s[a] = part
                d = _dot(p.astype(BF16), jnp.where(half[a], vt, zero_v))
                pv = d if pv is None else pv + d
            if online:
                acc_s[...] = acc_s[...] * jnp.where(lo, alphas[0], alphas[1]) + pv
            else:
                acc_s[...] += pv

        if online:
            m_s[...] = jnp.full(m_s.shape, NEG_BIG, F32)
            l_s[...] = jnp.zeros(l_s.shape, F32)
        else:
            lp_s[...] = jnp.zeros(lp_s.shape, F32)
        acc_s[...] = jnp.zeros(acc_s.shape, F32)

        if has_meta:
            tile(lambda a: km_ref[a], vm_ref[...], None)

        def body(j, carry):
            off = pl.multiple_of(j * t, t)
            tile(lambda a: k_ref[a, 0, pl.ds(off, t), :], v_ref[0, pl.ds(off, t), :], None)
            return carry

        lax.fori_loop(0, i, body, 0)

        off = pl.multiple_of(i * t, t)
        row = lax.broadcasted_iota(jnp.int32, (t, t), 0)
        col = lax.broadcasted_iota(jnp.int32, (t, t), 1)
        tile(lambda a: k_ref[a, 0, pl.ds(off, t), :], v_ref[0, pl.ds(off, t), :], col <= row)

        if online:
            inv = jnp.where(lo, 1.0 / l_s[0], 1.0 / l_s[1])
        else:
            inv = jnp.where(lo, 1.0 / jnp.sum(lp_s[0], axis=1, keepdims=True),
                            1.0 / jnp.sum(lp_s[1], axis=1, keepdims=True))
        sig, _ = _sigmoid_pair(gate_ref[0])
        o_ref[0] = (acc_s[...] * inv * sig).astype(BF16)

    @pl.when(flag_ref[0] == 1)
    def _():
        run(False)

    @pl.when(flag_ref[0] != 1)
    def _():
        run(True)


def _fox_attn(flag, q, k, v, gate, meta=None):
    b, l, d = v.shape
    t = _row_tile(l, 512)
    lp = LANES if t % LANES == 0 else t
    in_specs = [
        pl.BlockSpec(memory_space=pltpu.SMEM),
        pl.BlockSpec((2, 1, t, LANES), lambda bi, hp, i: (hp, bi, i, 0)),
        pl.BlockSpec((2, 1, l, LANES), lambda bi, hp, i: (hp, bi, 0, 0)),
        pl.BlockSpec((1, l, LANES), lambda bi, hp, i: (bi, 0, hp)),
        pl.BlockSpec((1, t, LANES), lambda bi, hp, i: (bi, i, hp)),
    ]
    args = [flag, q, k, v, gate]
    if meta is not None:
        km, vm = meta
        in_specs += [
            pl.BlockSpec((2, LANES, LANES), lambda bi, hp, i: (hp, 0, 0)),
            pl.BlockSpec((LANES, LANES), lambda bi, hp, i: (0, hp)),
        ]
        args += [km, vm]
    return pl.pallas_call(
        functools.partial(_fox_attn_kernel, meta is not None, t, lp),
        grid=(b, FOX_PAIRS, l // t),
        in_specs=in_specs,
        out_specs=pl.BlockSpec((1, t, LANES), lambda bi, hp, i: (bi, i, hp)),
        out_shape=jax.ShapeDtypeStruct((b, l, d), BF16),
        scratch_shapes=[
            pltpu.VMEM((2, t, 1), F32),
            pltpu.VMEM((2, t, 1), F32),
            pltpu.VMEM((2, t, lp), F32),
            pltpu.VMEM((t, LANES), F32),
        ],
        compiler_params=_params("arbitrary", "arbitrary", "arbitrary"),
        name="fox_attn",
    )(*args)


def _out_proj_kernel(y_ref, w_ref, h_ref, o_ref):
    o_ref[...] = h_ref[...] + _dot(y_ref[...], w_ref[...])


def _out_proj(y, w, h):
    m, d = h.shape
    tm = _row_tile(m, 512)
    row = lambda i: (i, 0)
    return pl.pallas_call(
        _out_proj_kernel,
        grid=(m // tm,),
        in_specs=[pl.BlockSpec((tm, d), row), pl.BlockSpec((d, d), lambda i: (0, 0)),
                  pl.BlockSpec((tm, d), row)],
        out_specs=pl.BlockSpec((tm, d), row),
        out_shape=jax.ShapeDtypeStruct((m, d), F32),
        compiler_params=_params("arbitrary"),
        name="out_proj",
    )(y, w, h)


def _ffn_kernel(final, h_ref, gain_ref, wg_ref, wu_ref, wo_ref, fgain_ref, o_ref, xn_s, acc_s):
    j = pl.program_id(1)

    @pl.when(j == 0)
    def _():
        xn_s[...] = _rms_rows(h_ref[...], gain_ref[...]).astype(BF16)
        acc_s[...] = jnp.zeros(acc_s.shape, F32)

    xn = xn_s[...]
    g = _dot(xn, wg_ref[...])
    u = _dot(xn, wu_ref[...])
    sig, _ = _sigmoid_pair(g)
    acc_s[...] += _dot((g * sig * u).astype(BF16), wo_ref[...])

    @pl.when(j == pl.num_programs(1) - 1)
    def _():
        out = h_ref[...] + acc_s[...]
        if final:
            out = _rms_rows(out, fgain_ref[...])
        o_ref[...] = out


def _ffn(h, gain, w_in, w_out, fgain, final):
    m, d = h.shape
    tm = _row_tile(m, 1024)
    nh = FFN_HIDDEN // FFN_TILE
    row = lambda i, j: (i, 0)
    const = lambda i, j: (0, 0)
    return pl.pallas_call(
        functools.partial(_ffn_kernel, final),
        grid=(m // tm, nh),
        in_specs=[
            pl.BlockSpec((tm, d), row),
            pl.BlockSpec((1, d), const),
            pl.BlockSpec((d, FFN_TILE), lambda i, j: (0, j)),
            pl.BlockSpec((d, FFN_TILE), lambda i, j: (0, j + nh)),
            pl.BlockSpec((FFN_TILE, d), lambda i, j: (j, 0)),
            pl.BlockSpec((1, d), const),
        ],
        out_specs=pl.BlockSpec((tm, d), row),
        out_shape=jax.ShapeDtypeStruct((m, d), F32),
        scratch_shapes=[pltpu.VMEM((tm, d), BF16), pltpu.VMEM((tm, d), F32)],
        compiler_params=_params("arbitrary", "arbitrary"),
        name="ffn",
    )(h, gain, w_in, w_in, w_out, fgain)


def _hgrn_proj_kernel(x_ref, gain_ref, w_ref, q_out, z_out, i_out, g_out, xn_s):
    j = pl.program_id(1)

    @pl.when(j == 0)
    def _():
        xn_s[...] = _rms_rows(x_ref[...], gain_ref[...]).astype(BF16)

    acc = _dot(xn_s[...], w_ref[...])

    @pl.when(j == 0)
    def _():
        q_out[...] = acc.astype(BF16)

    @pl.when(j == 1)
    def _():
        z_out[...] = acc

    @pl.when(j == 2)
    def _():
        i_out[...] = acc.astype(BF16)

    @pl.when(j == 3)
    def _():
        g_out[...] = acc.astype(BF16)


def _hgrn_proj(x, gain, w):
    m, d = x.shape
    tm = _row_tile(m, 512)
    row = lambda i, j: (i, 0)
    return pl.pallas_call(
        _hgrn_proj_kernel,
        grid=(m // tm, 4),
        in_specs=[pl.BlockSpec((tm, d), row), pl.BlockSpec((1, d), lambda i, j: (0, 0)),
                  pl.BlockSpec((d, d), lambda i, j: (0, j))],
        out_specs=[pl.BlockSpec((tm, d), row)] * 4,
        out_shape=[jax.ShapeDtypeStruct((m, d), BF16), jax.ShapeDtypeStruct((m, d), F32),
                   jax.ShapeDtypeStruct((m, d), BF16), jax.ShapeDtypeStruct((m, d), BF16)],
        scratch_shapes=[pltpu.VMEM((tm, d), BF16)],
        compiler_params=_params("arbitrary", "arbitrary"),
        name="hgrn_proj",
    )(x, gain, w)


def _hgrn_scan_kernel(c, q_ref, z_ref, i_ref, g_ref, lbp_ref, gg_ref, tri_ref, s0_ref,
                      y_ref, sfin_ref, st_s, b_s, k_s, v_s, q_s, o_s):
    step = pl.program_id(1)

    @pl.when(step == 0)
    def _():
        st_s[...] = s0_ref[...]

    lbp = lbp_ref[...]
    mx = jnp.maximum(lbp[0:1], lbp[1:2])
    e0 = jnp.exp(lbp[0:1] - mx)
    e1 = jnp.exp(lbp[1:2] - mx)
    den = e0 + e1
    sm0 = e0 / den
    lb = (sm0 + e1 / den) - sm0

    sig, sig_neg = _sigmoid_pair(z_ref[0])
    log_f = jnp.log(lb + (1.0 - lb) * sig)
    kk = (1.0 - lb) * sig_neg
    qz = q_ref[0].astype(F32)
    qq = qz * _sigmoid_pair(qz)[0]
    vv = i_ref[0].astype(F32)

    b = _dot(tri_ref[...], log_f, prec=HIGHEST)
    b_last = b[c - 1:c, :]
    mid = c // 2
    b_mid = b[mid - 1:mid, :]
    spread = jnp.maximum(jnp.max(b[0:1, :] - b_mid), jnp.max(b_mid - b_last))

    q_in = (qq * jnp.exp(b)).astype(BF16)
    k_out = (kk * jnp.exp(b_last - b)).astype(BF16)
    v_b = vv.astype(BF16)
    decay = jnp.exp(b_last)

    row = lax.broadcasted_iota(jnp.int32, (c, c), 0)
    col = lax.broadcasted_iota(jnp.int32, (c, c), 1)
    causal = col <= row

    @pl.when(spread <= FAST_RANGE)
    def _():
        qf = (qq * jnp.exp(b - b_mid)).astype(BF16)
        kf = (kk * jnp.exp(b_mid - b)).astype(BF16)
        for h in range(HGRN_HEADS):
            sl = slice(h * HGRN_DK, (h + 1) * HGRN_DK)
            a = jnp.where(causal, _dot(qf[:, sl], kf[:, sl], NT), 0.0)
            o_s[:, sl] = _dot(a.astype(BF16), v_b[:, sl])

    @pl.when(spread > FAST_RANGE)
    def _():
        b_s[...] = b
        k_s[...] = kk
        v_s[...] = vv
        q_s[...] = qq
        o_s[...] = jnp.zeros(o_s.shape, F32)
        rowc = lax.broadcasted_iota(jnp.int32, (c, 1), 0)

        def body(s, carry):
            bs = b_s[pl.ds(s, 1), :]
            p = q_s[...] * jnp.exp(jnp.minimum(b_s[...] - bs, 0.0)) * k_s[pl.ds(s, 1), :]
            vs = v_s[pl.ds(s, 1), :]
            seen = rowc >= s
            for h in range(HGRN_HEADS):
                sl = slice(h * HGRN_DK, (h + 1) * HGRN_DK)
                w = jnp.where(seen, jnp.sum(p[:, sl], axis=1, keepdims=True), 0.0)
                o_s[:, sl] += w * vs[:, sl]
            return carry

        lax.fori_loop(0, c, body, 0)

    gsig = _sigmoid_pair(g_ref[0].astype(F32))[0]
    gate = g_ref[0].astype(F32) * gsig
    for h in range(HGRN_HEADS):
        sl = slice(h * HGRN_DK, (h + 1) * HGRN_DK)
        st = st_s[h]
        o = o_s[:, sl] + _dot(q_in[:, sl], st.astype(BF16), NT)
        st_s[h] = st * decay[:, sl] + _dot(v_b[:, sl], k_out[:, sl], TN)
        y = _rms_rows(o, gg_ref[:, sl]) * gate[:, sl]
        y_ref[0, :, sl] = y.astype(BF16)

    @pl.when(step == pl.num_programs(1) - 1)
    def _():
        sfin_ref[...] = st_s[...]


def _hgrn_scan(q, z, i, g, lbp, gg, s0):
    b, l, d = q.shape
    c = _row_tile(l, 64)
    tri = jnp.tril(jnp.ones((c, c), F32))
    blk = lambda bi, t: (bi, t, 0)
    const2 = lambda bi, t: (0, 0)
    const3 = lambda bi, t: (0, 0, 0)
    st_shape = (HGRN_HEADS, HGRN_DK, HGRN_DK)
    return pl.pallas_call(
        functools.partial(_hgrn_scan_kernel, c),
        grid=(b, l // c),
        in_specs=[
            pl.BlockSpec((1, c, d), blk),
            pl.BlockSpec((1, c, d), blk),
            pl.BlockSpec((1, c, d), blk),
            pl.BlockSpec((1, c, d), blk),
            pl.BlockSpec((2, d), const2),
            pl.BlockSpec((1, d), const2),
            pl.BlockSpec((c, c), const2),
            pl.BlockSpec(st_shape, const3),
        ],
        out_specs=[pl.BlockSpec((1, c, d), blk), pl.BlockSpec(st_shape, const3)],
        out_shape=[jax.ShapeDtypeStruct((b, l, d), BF16), jax.ShapeDtypeStruct(st_shape, F32)],
        scratch_shapes=[pltpu.VMEM(st_shape, F32)] + [pltpu.VMEM((c, d), F32)] * 5,
        compiler_params=_params("arbitrary", "arbitrary"),
        name="hgrn_scan",
    )(q, z, i, g, lbp, gg, tri, s0)


def kernel(x, meta_tokens, attn_norm, ffn_norm, final_norm, fox_w_in, fox_b_f, fox_q_norm, fox_k_norm,
           fox_w_out, hgrn_w_in, hgrn_lower_bounds, hgrn_g_norm, hgrn_w_out, ffn_w_in, ffn_w_out):
    bsz, seq, d = x.shape
    row = lambda v: v.reshape(1, -1).astype(F32)

    w_fox = fox_w_in[0, :, :4 * d].astype(BF16)
    w_f = jnp.pad(fox_w_in[0, :, 4 * d:], ((0, 0), (0, LANES - FOX_HEADS))).astype(BF16)
    w_fox_out = fox_w_out[0].astype(BF16)
    w_hgrn = hgrn_w_in[0].astype(BF16)
    w_hgrn_out = hgrn_w_out[0].astype(BF16)
    w_ffn_in = ffn_w_in.astype(BF16)
    w_ffn_out = ffn_w_out.astype(BF16)
    head_of = jnp.arange(d) // FOX_HEAD_DIM
    bd = (head_of[:, None] == jnp.arange(LANES)[None, :]).astype(BF16)
    ex = bd.T
    qg = row(jnp.tile(fox_q_norm[0], FOX_HEADS)) * (FOX_HEAD_DIM ** -0.5 * LOG2E)
    kg = row(jnp.tile(fox_k_norm[0], FOX_HEADS))
    bound2 = (1.01 * LOG2E * FOX_HEAD_DIM ** 0.5 * jnp.max(jnp.abs(fox_q_norm[0]))
              * jnp.max(jnp.abs(fox_k_norm[0]))).astype(BF16).astype(F32)
    flag = (bound2 <= BOUND_LIMIT).astype(jnp.int32).reshape(1)
    pq, pk = _placement(bound2)
    b_f = jnp.pad(row(fox_b_f[0]), ((0, 0), (0, LANES - FOX_HEADS)))
    gg = row(jnp.tile(hgrn_g_norm[0], HGRN_HEADS))
    lbp = hgrn_lower_bounds.astype(F32)

    h_meta = meta_tokens.astype(F32)
    h_real = x.reshape(bsz * seq, d)
    seq3 = lambda a: a.reshape(bsz, seq, -1)

    gain0 = row(attn_norm[0])
    cm = _fox_gate(h_meta[None], gain0, w_f, b_f, jnp.zeros((1, LANES), F32))
    cr = _fox_gate(seq3(h_real), gain0, w_f, b_f, cm[0, N_META - 1:N_META])
    proj = lambda h, c: _fox_proj(h, gain0, w_fox, c, bd, ex, qg, kg, pq, pk)
    qm, km, vm, gm = proj(h_meta, cm[0])
    qr, kr, vr, gr = proj(h_real, cr.reshape(bsz * seq, LANES))
    ym = _fox_attn(flag, qm[:, None], km[:, None], vm[None], gm[None])[0]
    bound_lane = jnp.array([(FOX_HEAD_DIM if h % 2 == 0 else 0) + AUG_BOUND for h in range(FOX_HEADS)])
    km_pad = jnp.zeros((FOX_HEADS, LANES, LANES), BF16).at[:, :N_META].set(km)
    km_pad = km_pad.at[jnp.arange(FOX_HEADS), N_META:, bound_lane].set(PAD_KEY_BIAS)
    vm_pad = jnp.pad(vm, ((0, LANES - N_META), (0, 0)))
    heads4 = lambda a: a.reshape(FOX_HEADS, bsz, seq, LANES)
    yr = _fox_attn(flag, heads4(qr), heads4(kr), seq3(vr), seq3(gr),
                   meta=(km_pad, vm_pad)).reshape(bsz * seq, d)
    h_meta = _out_proj(ym, w_fox_out, h_meta)
    h_real = _out_proj(yr, w_fox_out, h_real)
    ffn0 = lambda h: _ffn(h, row(ffn_norm[0]), w_ffn_in[0], w_ffn_out[0], row(final_norm), False)
    h_meta = ffn0(h_meta)
    h_real = ffn0(h_real)

    proj = lambda h: _hgrn_proj(h, row(attn_norm[1]), w_hgrn)
    qm, zm, im, gm = proj(h_meta)
    qr, zr, ir, gr = proj(h_real)
    s0 = jnp.zeros((HGRN_HEADS, HGRN_DK, HGRN_DK), F32)
    _, s_meta = _hgrn_scan(qm[None], zm[None], im[None], gm[None], lbp, gg, s0)
    yr, _ = _hgrn_scan(seq3(qr), seq3(zr), seq3(ir), seq3(gr), lbp, gg, s_meta)
    h_real = _out_proj(yr.reshape(bsz * seq, d), w_hgrn_out, h_real)
    out = _ffn(h_real, row(ffn_norm[1]), w_ffn_in[1], w_ffn_out[1], row(final_norm), True)
    return out.reshape(bsz, seq, d)
```

```python
import functools
import math

import numpy as np
import jax
import jax.numpy as jnp
from jax import lax
from jax.experimental import pallas as pl
from jax.experimental.pallas import tpu as pltpu

D_MODEL = 1024
N_META = 16
FOX_HEADS = 16
FOX_HEAD_DIM = 64
HGRN_HEADS = 8
HGRN_DK = 128
HGRN_CHUNK = 64
FFN_HIDDEN = 2816
EPS = 1e-6

LANES = 128
FOX_PAIRS = FOX_HEADS // 2
FFN_TILE = 256
NEG_BIG = -1e30
PAD_KEY_BIAS = -30000.0
LOG2E = math.log2(math.e)
BOUND_LIMIT = 56.0
FAST_RANGE = 40.0
VMEM_LIMIT = 52 * 1024 * 1024

ROW_TILE = 512
ATTN_TILE = 512
SCAN_TILE = 256
GATE_TILE = 512

AUG_C = 0
AUG_ONE = 3
AUG_BOUND = 6
C3_MID, C3_LO, C3_ONE = 16, 32, 48

F32 = jnp.float32
BF16 = jnp.bfloat16

NN = (((1,), (0,)), ((), ()))
NT = (((1,), (1,)), ((), ()))
TN = (((0,), (0,)), ((), ()))


def _dot(a, b, dims=NN):
    return lax.dot_general(a, b, dims, preferred_element_type=F32)


def _params(*sem):
    return pltpu.CompilerParams(dimension_semantics=sem, vmem_limit_bytes=VMEM_LIMIT)


def _row_tile(m, pref):
    return pref if m % pref == 0 else m


def _resident(shape):
    return pl.BlockSpec(shape, lambda *_: (0,) * len(shape), pipeline_mode=pl.Buffered(1))


def _sigmoid(x):
    return 1.0 / (1.0 + jnp.exp(-x))


def _rms_rows(x, gain):
    ms = jnp.mean(x * x, axis=-1, keepdims=True)
    return x * lax.rsqrt(ms + EPS) * gain


def _split3(x):
    hi = x.astype(BF16)
    r1 = x - hi.astype(F32)
    mid = r1.astype(BF16)
    return hi, mid, (r1 - mid.astype(F32)).astype(BF16)


def _tri_cumsum(tri, x, pieces):
    if pieces == 3:
        hi, mid, lo = _split3(x)
        return _dot(tri, hi) + _dot(tri, mid) + _dot(tri, lo)
    hi = x.astype(BF16)
    return _dot(tri, hi) + _dot(tri, (x - hi.astype(F32)).astype(BF16))


def _block_tril(t, c):
    r = np.arange(t)
    return jnp.asarray((r[None, :] <= r[:, None]) & (r[None, :] // c == r[:, None] // c), BF16)


def _fox_gate_kernel(x_ref, gain_ref, wf_ref, bias_ref, c0_ref, tri_ref, c_out, carry_s):
    @pl.when(pl.program_id(1) == 0)
    def _():
        carry_s[...] = c0_ref[...]

    xn = _rms_rows(x_ref[0], gain_ref[...]).astype(BF16)
    zz = _dot(xn, wf_ref[...]) + bias_ref[...]
    log_f = jnp.minimum(zz, 0.0) - jnp.log1p(jnp.exp(-jnp.abs(zz)))
    cs = _tri_cumsum(tri_ref[...], log_f, 3) + carry_s[...]
    c_out[0] = cs
    tc = cs.shape[0]
    carry_s[...] = cs[tc - 1:tc, :]


def _fox_gate(x, gain, wf, bias, c0):
    b, l, d = x.shape
    tc = _row_tile(l, GATE_TILE)
    const = lambda i, t: (0, 0)
    return pl.pallas_call(
        _fox_gate_kernel,
        grid=(b, l // tc),
        in_specs=[
            pl.BlockSpec((1, tc, d), lambda i, t: (i, t, 0)),
            pl.BlockSpec((1, d), const),
            pl.BlockSpec((d, LANES), const),
            pl.BlockSpec((1, LANES), const),
            pl.BlockSpec((1, LANES), const),
            pl.BlockSpec((tc, tc), const),
        ],
        out_specs=pl.BlockSpec((1, tc, LANES), lambda i, t: (i, t, 0)),
        out_shape=jax.ShapeDtypeStruct((b, l, LANES), F32),
        scratch_shapes=[pltpu.VMEM((1, LANES), F32)],
        compiler_params=_params("arbitrary", "arbitrary"),
        name="fox_gate",
    )(x, gain, wf, bias, c0, _block_tril(tc, tc))


def _fox_proj_kernel(x_ref, gain_ref, w_ref, c_ref, bd_ref, ex_ref, qg_ref, kg_ref, pq_ref, pk_ref,
                     q_out, k_out, v_out, g_out):
    xn = _rms_rows(x_ref[...], gain_ref[...]).astype(BF16)
    lane = lax.broadcasted_iota(jnp.int32, (1, LANES), 1)
    lo = lane < FOX_HEAD_DIM

    cc = jnp.where(lane < FOX_HEADS, c_ref[...] * LOG2E, 0.0)
    hi, mid, low = _split3(cc)
    c3 = (hi.astype(F32) + pltpu.roll(mid.astype(F32), C3_MID, axis=1)
          + pltpu.roll(low.astype(F32), C3_LO, axis=1) + jnp.where(lane == C3_ONE, 1.0, 0.0)).astype(BF16)

    def head_slabs(acc, gain, place_ref, out):
        ss = _dot((acc * acc).astype(BF16), bd_ref[...])
        r = lax.rsqrt(ss * (1.0 / FOX_HEAD_DIM) + EPS)
        r_hi = r.astype(BF16)
        r_lo = (r - r_hi.astype(F32)).astype(BF16)
        xh = acc * _dot(jnp.concatenate([r_hi, r_lo], axis=1), ex_ref[...]) * gain
        aug = _dot(c3, place_ref[...])
        for p in range(FOX_PAIRS):
            pair = slice(p * LANES, (p + 1) * LANES)
            out[2 * p] = jnp.where(lo, xh[:, pair], aug[:, pair]).astype(BF16)
            out[2 * p + 1] = jnp.where(lo, aug[:, pair], xh[:, pair]).astype(BF16)

    head_slabs(_dot(xn, w_ref[0]), qg_ref[...], pq_ref, q_out)
    head_slabs(_dot(xn, w_ref[1]), kg_ref[...], pk_ref, k_out)
    v_out[...] = _dot(xn, w_ref[2]).astype(BF16)
    g_out[...] = _dot(xn, w_ref[3])


def _fox_proj(x, gain, w, c, bd, ex, qg, kg, pq, pk):
    m = x.shape[0]
    tm = _row_tile(m, ROW_TILE)
    d = D_MODEL
    row = lambda i: (i, 0)
    slab = pl.BlockSpec((FOX_HEADS, tm, LANES), lambda i: (0, i, 0))
    return pl.pallas_call(
        _fox_proj_kernel,
        grid=(m // tm,),
        in_specs=[
            pl.BlockSpec((tm, d), row),
            _resident((1, d)),
            _resident((4, d, d)),
            pl.BlockSpec((tm, LANES), row),
            _resident((d, LANES)),
            _resident((2 * LANES, d)),
            _resident((1, d)),
            _resident((1, d)),
            _resident((LANES, d)),
            _resident((LANES, d)),
        ],
        out_specs=[slab, slab, pl.BlockSpec((tm, d), row), pl.BlockSpec((tm, d), row)],
        out_shape=[
            jax.ShapeDtypeStruct((FOX_HEADS, m, LANES), BF16),
            jax.ShapeDtypeStruct((FOX_HEADS, m, LANES), BF16),
            jax.ShapeDtypeStruct((m, d), BF16),
            jax.ShapeDtypeStruct((m, d), F32),
        ],
        compiler_params=_params("arbitrary"),
        name="fox_proj",
    )(x, gain, w, c, bd, ex, qg, kg, pq, pk)


def _bound_lane(h):
    return (FOX_HEAD_DIM if h % 2 == 0 else 0) + AUG_BOUND


def _placement(bound2):
    pq = np.zeros((LANES, D_MODEL), np.float32)
    pk = np.zeros((LANES, D_MODEL), np.float32)
    pb = np.zeros((LANES, D_MODEL), np.float32)
    for h in range(FOX_HEADS):
        base = (h // 2) * LANES + (FOX_HEAD_DIM if h % 2 == 0 else 0)
        for piece, src in enumerate((h, C3_MID + h, C3_LO + h)):
            pq[src, base + AUG_C + piece] = 1.0
            pk[src, base + AUG_ONE + piece] = -1.0
            pq[C3_ONE, base + AUG_ONE + piece] = 1.0
            pk[C3_ONE, base + AUG_C + piece] = 1.0
        pq[C3_ONE, base + AUG_BOUND] = 1.0
        pb[C3_ONE, base + AUG_BOUND] = 1.0
    return jnp.asarray(pq, BF16), (jnp.asarray(pk) - jnp.asarray(pb) * bound2).astype(BF16)


def _fox_attn_kernel(has_meta, t, nt, lp, flag_ref, q_ref, k_ref, v_ref, gate_ref, *rest):
    if has_meta:
        km_ref, vm_ref, o_ref, m_s, l_s, lp_s, acc_s = rest
    else:
        o_ref, m_s, l_s, lp_s, acc_s = rest
    i = pl.program_id(2)

    lane = lax.broadcasted_iota(jnp.int32, (1, LANES), 1)
    lo = lane < FOX_HEAD_DIM
    half = (lo, jnp.logical_not(lo))

    def run(online):
        def tile(keys, vt, mask):
            alphas, pv = [], None
            zero_v = jnp.zeros_like(vt)
            for a in range(2):
                s = _dot(q_ref[a, 0], keys(a), NT)
                if mask is not None:
                    s = jnp.where(mask, s, NEG_BIG)
                if online:
                    m_prev = m_s[a]
                    m_new = jnp.maximum(m_prev, jnp.max(s, axis=1, keepdims=True))
                    alpha = jnp.exp2(m_prev - m_new)
                    p = jnp.exp2(s - m_new)
                    l_s[a] = alpha * l_s[a] + jnp.sum(p, axis=1, keepdims=True)
                    m_s[a] = m_new
                    alphas.append(alpha)
                else:
                    p = jnp.exp2(s)
                    part = lp_s[a]
                    for c0 in range(0, p.shape[1], lp):
                        part = part + p[:, c0:c0 + lp]
                    lp_s[a] = part
                d = _dot(p.astype(BF16), jnp.where(half[a], vt, zero_v))
                pv = d if pv is None else pv + d
            if online:
                acc_s[...] = acc_s[...] * jnp.where(lo, alphas[0], alphas[1]) + pv
            else:
                acc_s[...] += pv

        def real_tile(off, size, mask):
            tile(lambda a: k_ref[a, 0, pl.ds(off, size), :], v_ref[0, pl.ds(off, size), :], mask)

        if online:
            m_s[...] = jnp.full(m_s.shape, NEG_BIG, F32)
            l_s[...] = jnp.zeros(l_s.shape, F32)
        else:
            lp_s[...] = jnp.zeros(lp_s.shape, F32)
        acc_s[...] = jnp.zeros(acc_s.shape, F32)

        if nt >= 2:
            def body(j, carry):
                real_tile(pl.multiple_of(j * (2 * t), 2 * t), 2 * t, None)
                return carry

            lax.fori_loop(0, lax.shift_right_logical(i, 1), body, 0)

            @pl.when((i & 1) == 1)
            def _():
                real_tile(pl.multiple_of((i - 1) * t, t), t, None)

        off = pl.multiple_of(i * t, t)
        n_pre = LANES if has_meta else 0
        row = lax.broadcasted_iota(jnp.int32, (t, n_pre + t), 0)
        col = lax.broadcasted_iota(jnp.int32, (t, n_pre + t), 1)
        if has_meta:
            tile(lambda a: jnp.concatenate([km_ref[a], k_ref[a, 0, pl.ds(off, t), :]], axis=0),
                 jnp.concatenate([vm_ref[...], v_ref[0, pl.ds(off, t), :]], axis=0),
                 col - n_pre <= row)
        else:
            real_tile(off, t, col <= row)

        if online:
            inv = jnp.where(lo, 1.0 / l_s[0], 1.0 / l_s[1])
        else:
            inv = jnp.where(lo, 1.0 / jnp.sum(lp_s[0], axis=1, keepdims=True),
                            1.0 / jnp.sum(lp_s[1], axis=1, keepdims=True))
        o_ref[0] = (acc_s[...] * inv * _sigmoid(gate_ref[0])).astype(BF16)

    @pl.when(flag_ref[0] == 1)
    def _():
        run(False)

    @pl.when(flag_ref[0] != 1)
    def _():
        run(True)


def _fox_attn(flag, q, k, v, gate, meta=None):
    b, l, d = v.shape
    t = _row_tile(l, ATTN_TILE)
    nt = l // t
    lp = LANES if t % LANES == 0 else t
    in_specs = [
        pl.BlockSpec(memory_space=pltpu.SMEM),
        pl.BlockSpec((2, 1, t, LANES), lambda bi, hp, i: (hp, bi, i, 0)),
        pl.BlockSpec((2, 1, l, LANES), lambda bi, hp, i: (hp, bi, 0, 0)),
        pl.BlockSpec((1, l, LANES), lambda bi, hp, i: (bi, 0, hp)),
        pl.BlockSpec((1, t, LANES), lambda bi, hp, i: (bi, i, hp)),
    ]
    args = [flag, q, k, v, gate]
    if meta is not None:
        km, vm = meta
        in_specs += [
            pl.BlockSpec((2, LANES, LANES), lambda bi, hp, i: (hp, 0, 0)),
            pl.BlockSpec((LANES, LANES), lambda bi, hp, i: (0, hp)),
        ]
        args += [km, vm]
    return pl.pallas_call(
        functools.partial(_fox_attn_kernel, meta is not None, t, nt, lp),
        grid=(b, FOX_PAIRS, nt),
        in_specs=in_specs,
        out_specs=pl.BlockSpec((1, t, LANES), lambda bi, hp, i: (bi, i, hp)),
        out_shape=jax.ShapeDtypeStruct((b, l, d), BF16),
        scratch_shapes=[
            pltpu.VMEM((2, t, 1), F32),
            pltpu.VMEM((2, t, 1), F32),
            pltpu.VMEM((2, t, lp), F32),
            pltpu.VMEM((t, LANES), F32),
        ],
        compiler_params=_params("arbitrary", "arbitrary", "arbitrary"),
        name="fox_attn",
    )(*args)


def _ffn_kernel(final, nh, y_ref, wmix_ref, h_ref, gain_ref, win_ref, wout_ref, fgain_ref, o_ref,
                hm_s, xn_s, acc_s):
    hm = h_ref[...] + _dot(y_ref[...], wmix_ref[...])
    hm_s[...] = hm
    xn_s[...] = _rms_rows(hm, gain_ref[...]).astype(BF16)
    acc_s[...] = jnp.zeros(acc_s.shape, F32)

    def body(j, carry):
        xn = xn_s[...]
        g = _dot(xn, win_ref[j])
        u = _dot(xn, win_ref[j + nh])
        acc_s[...] += _dot((g * _sigmoid(g) * u).astype(BF16), wout_ref[j])
        return carry

    lax.fori_loop(0, nh, body, 0, unroll=True)

    out = hm_s[...] + acc_s[...]
    if final:
        out = _rms_rows(out, fgain_ref[...])
    o_ref[...] = out


def _ffn(y, w_mix, h, gain, w_in, w_out, fgain, final):
    m, d = h.shape
    tm = _row_tile(m, ROW_TILE)
    nh = FFN_HIDDEN // FFN_TILE
    row = lambda i: (i, 0)
    return pl.pallas_call(
        functools.partial(_ffn_kernel, final, nh),
        grid=(m // tm,),
        in_specs=[
            pl.BlockSpec((tm, d), row),
            _resident((d, d)),
            pl.BlockSpec((tm, d), row),
            _resident((1, d)),
            _resident((2 * nh, d, FFN_TILE)),
            _resident((nh, FFN_TILE, d)),
            _resident((1, d)),
        ],
        out_specs=pl.BlockSpec((tm, d), row),
        out_shape=jax.ShapeDtypeStruct((m, d), F32),
        scratch_shapes=[pltpu.VMEM((tm, d), F32), pltpu.VMEM((tm, d), BF16), pltpu.VMEM((tm, d), F32)],
        compiler_params=_params("arbitrary"),
        name="ffn",
    )(y, w_mix, h, gain, w_in, w_out, fgain)


def _hgrn_proj_kernel(x_ref, gain_ref, w_ref, q_out, z_out, i_out, g_out):
    xn = _rms_rows(x_ref[...], gain_ref[...]).astype(BF16)
    q_out[...] = _dot(xn, w_ref[0]).astype(BF16)
    z_out[...] = _dot(xn, w_ref[1])
    i_out[...] = _dot(xn, w_ref[2]).astype(BF16)
    g_out[...] = _dot(xn, w_ref[3]).astype(BF16)


def _hgrn_proj(x, gain, w):
    m, d = x.shape
    tm = _row_tile(m, ROW_TILE)
    row = lambda i: (i, 0)
    return pl.pallas_call(
        _hgrn_proj_kernel,
        grid=(m // tm,),
        in_specs=[pl.BlockSpec((tm, d), row), _resident((1, d)), _resident((4, d, d))],
        out_specs=[pl.BlockSpec((tm, d), row)] * 4,
        out_shape=[jax.ShapeDtypeStruct((m, d), BF16), jax.ShapeDtypeStruct((m, d), F32),
                   jax.ShapeDtypeStruct((m, d), BF16), jax.ShapeDtypeStruct((m, d), BF16)],
        compiler_params=_params("arbitrary"),
        name="hgrn_proj",
    )(x, gain, w)


def _hgrn_scan_kernel(t, c, q_ref, z_ref, i_ref, g_ref, lbp_ref, gg_ref, tri_ref, s0_ref,
                      y_ref, sfin_ref, st_s, oi_s, oa_s, qin_s, kout_s, qf_s, kf_s, b_s, q_s, k_s, v_s):
    step = pl.program_id(1)
    nc = t // c
    mid = c // 2
    heads = [slice(h * HGRN_DK, (h + 1) * HGRN_DK) for h in range(HGRN_HEADS)]

    @pl.when(step == 0)
    def _():
        st_s[...] = s0_ref[...]

    lbp = lbp_ref[...]
    mx = jnp.maximum(lbp[0:1], lbp[1:2])
    e0 = jnp.exp(lbp[0:1] - mx)
    e1 = jnp.exp(lbp[1:2] - mx)
    den = e0 + e1
    sm0 = e0 / den
    lb = (sm0 + e1 / den) - sm0

    sig = _sigmoid(z_ref[0])
    log_f = jnp.log(lb + (1.0 - lb) * sig)
    kk = (1.0 - lb) * (1.0 - sig)
    qz = q_ref[0].astype(F32)
    qq = qz * _sigmoid(qz)
    v_b = i_ref[0]

    b = _tri_cumsum(tri_ref[...], log_f, 2)

    spread = jnp.float32(0.0)
    decay = []
    for ci in range(nc):
        rs = slice(ci * c, (ci + 1) * c)
        bc = b[rs]
        b_last = bc[c - 1:c]
        b_mid = bc[mid - 1:mid]
        spread = jnp.maximum(spread, jnp.maximum(jnp.max(bc[0:1] - b_mid), jnp.max(b_mid - b_last)))
        q_in = qq[rs] * jnp.exp(bc)
        k_out = kk[rs] * jnp.exp(b_last - bc)
        qin_s[rs] = q_in.astype(BF16)
        kout_s[rs] = k_out.astype(BF16)
        qf_s[rs] = (q_in * jnp.exp(-b_mid)).astype(BF16)
        kf_s[rs] = (k_out * jnp.exp(b_mid - b_last)).astype(BF16)
        decay.append(jnp.exp(b_last))

    keep = tri_ref[...] > 0
    for sl in heads:
        a = jnp.where(keep, _dot(qf_s[:, sl], kf_s[:, sl], NT), 0.0)
        oa_s[:, sl] = _dot(a.astype(BF16), v_b[:, sl])

    for ci in range(nc):
        rs = slice(ci * c, (ci + 1) * c)
        for h, sl in enumerate(heads):
            st = st_s[h]
            oi_s[rs, sl] = _dot(qin_s[rs, sl], st.astype(BF16), NT)
            st_s[h] = st * decay[ci][:, sl] + _dot(v_b[rs, sl], kout_s[rs, sl], TN)

    @pl.when(spread > FAST_RANGE)
    def _():
        b_s[...] = b
        k_s[...] = kk
        q_s[...] = qq
        v_s[...] = v_b.astype(F32)
        oa_s[...] = jnp.zeros(oa_s.shape, F32)
        rowc = lax.broadcasted_iota(jnp.int32, (c, 1), 0)

        def body(s, carry):
            base = pl.multiple_of((s // c) * c, c)
            blk = pl.ds(base, c)
            p = (q_s[blk, :] * jnp.exp(jnp.minimum(b_s[blk, :] - b_s[pl.ds(s, 1), :], 0.0))
                 * k_s[pl.ds(s, 1), :])
            vs = v_s[pl.ds(s, 1), :]
            seen = rowc + base >= s
            for sl in heads:
                w = jnp.where(seen, jnp.sum(p[:, sl], axis=1, keepdims=True), 0.0)
                oa_s[blk, sl] += w * vs[:, sl]
            return carry

        lax.fori_loop(0, t, body, 0)

    gz = g_ref[0].astype(F32)
    gate = gz * _sigmoid(gz)
    o = oi_s[...] + oa_s[...]
    for sl in heads:
        y_ref[0, :, sl] = (_rms_rows(o[:, sl], gg_ref[:, sl]) * gate[:, sl]).astype(BF16)

    @pl.when(step == pl.num_programs(1) - 1)
    def _():
        sfin_ref[...] = st_s[...]


def _hgrn_scan(q, z, i, g, lbp, gg, s0):
    b, l, d = q.shape
    t = _row_tile(l, SCAN_TILE)
    c = _row_tile(t, HGRN_CHUNK)
    blk = lambda bi, s: (bi, s, 0)
    st_shape = (HGRN_HEADS, HGRN_DK, HGRN_DK)
    return pl.pallas_call(
        functools.partial(_hgrn_scan_kernel, t, c),
        grid=(b, l // t),
        in_specs=[
            pl.BlockSpec((1, t, d), blk),
            pl.BlockSpec((1, t, d), blk),
            pl.BlockSpec((1, t, d), blk),
            pl.BlockSpec((1, t, d), blk),
            _resident((2, d)),
            _resident((1, d)),
            _resident((t, t)),
            _resident(st_shape),
        ],
        out_specs=[pl.BlockSpec((1, t, d), blk), pl.BlockSpec(st_shape, lambda bi, s: (0, 0, 0))],
        out_shape=[jax.ShapeDtypeStruct((b, l, d), BF16), jax.ShapeDtypeStruct(st_shape, F32)],
        scratch_shapes=([pltpu.VMEM(st_shape, F32)] + [pltpu.VMEM((t, d), F32)] * 2
                        + [pltpu.VMEM((t, d), BF16)] * 4 + [pltpu.VMEM((t, d), F32)] * 4),
        compiler_params=_params("arbitrary", "arbitrary"),
        name="hgrn_scan",
    )(q, z, i, g, lbp, gg, _block_tril(t, c), s0)


def kernel(x, meta_tokens, attn_norm, ffn_norm, final_norm, fox_w_in, fox_b_f, fox_q_norm, fox_k_norm,
           fox_w_out, hgrn_w_in, hgrn_lower_bounds, hgrn_g_norm, hgrn_w_out, ffn_w_in, ffn_w_out):
    bsz, seq, d = x.shape
    row = lambda v: v.reshape(1, -1).astype(F32)
    nh = FFN_HIDDEN // FFN_TILE

    col_tiles = lambda w, n: w.astype(BF16).reshape(d, n, -1).transpose(1, 0, 2)
    w_fox = col_tiles(fox_w_in[0, :, :4 * d], 4)
    w_f = jnp.pad(fox_w_in[0, :, 4 * d:], ((0, 0), (0, LANES - FOX_HEADS))).astype(BF16)
    w_fox_out = fox_w_out[0].astype(BF16)
    w_hgrn = col_tiles(hgrn_w_in[0], 4)
    w_hgrn_out = hgrn_w_out[0].astype(BF16)
    w_ffn_in = [col_tiles(ffn_w_in[i], 2 * nh) for i in range(2)]
    w_ffn_out = ffn_w_out.astype(BF16).reshape(2, nh, FFN_TILE, d)
    head_of = np.arange(d) // FOX_HEAD_DIM
    bd = jnp.asarray(head_of[:, None] == np.arange(LANES)[None, :], BF16)
    ex = jnp.concatenate([bd.T, bd.T], axis=0)
    qg = row(jnp.tile(fox_q_norm[0], FOX_HEADS)) * (FOX_HEAD_DIM ** -0.5 * LOG2E)
    kg = row(jnp.tile(fox_k_norm[0], FOX_HEADS))
    bound2 = (1.01 * LOG2E * FOX_HEAD_DIM ** 0.5 * jnp.max(jnp.abs(fox_q_norm[0]))
              * jnp.max(jnp.abs(fox_k_norm[0]))).astype(BF16).astype(F32)
    flag = (bound2 <= BOUND_LIMIT).astype(jnp.int32).reshape(1)
    pq, pk = _placement(bound2)
    b_f = jnp.pad(row(fox_b_f[0]), ((0, 0), (0, LANES - FOX_HEADS)))
    gg = row(jnp.tile(hgrn_g_norm[0], HGRN_HEADS))
    lbp = hgrn_lower_bounds.astype(F32)

    h_meta = meta_tokens.astype(F32)
    h_real = x.reshape(bsz * seq, d)
    seq3 = lambda a: a.reshape(bsz, seq, -1)

    gain0 = row(attn_norm[0])
    cm = _fox_gate(h_meta[None], gain0, w_f, b_f, jnp.zeros((1, LANES), F32))
    cr = _fox_gate(seq3(h_real), gain0, w_f, b_f, cm[0, N_META - 1:N_META])
    proj = lambda h, c: _fox_proj(h, gain0, w_fox, c, bd, ex, qg, kg, pq, pk)
    qm, km, vm, gm = proj(h_meta, cm[0])
    qr, kr, vr, gr = proj(h_real, cr.reshape(bsz * seq, LANES))
    ym = _fox_attn(flag, qm[:, None], km[:, None], vm[None], gm[None])[0]
    pad_bias = np.zeros((FOX_HEADS, LANES, LANES), np.float32)
    for h in range(FOX_HEADS):
        pad_bias[h, N_META:, _bound_lane(h)] = PAD_KEY_BIAS
    km_pad = jnp.pad(km, ((0, 0), (0, LANES - N_META), (0, 0))) + jnp.asarray(pad_bias, BF16)
    vm_pad = jnp.pad(vm, ((0, LANES - N_META), (0, 0)))
    heads4 = lambda a: a.reshape(FOX_HEADS, bsz, seq, LANES)
    yr = _fox_attn(flag, heads4(qr), heads4(kr), seq3(vr), seq3(gr),
                   meta=(km_pad, vm_pad)).reshape(bsz * seq, d)
    ffn0 = lambda y, h: _ffn(y, w_fox_out, h, row(ffn_norm[0]), w_ffn_in[0], w_ffn_out[0],
                             row(final_norm), False)
    h_meta = ffn0(ym, h_meta)
    h_real = ffn0(yr, h_real)

    proj = lambda h: _hgrn_proj(h, row(attn_norm[1]), w_hgrn)
    qm, zm, im, gm = proj(h_meta)
    qr, zr, ir, gr = proj(h_real)
    s0 = jnp.zeros((HGRN_HEADS, HGRN_DK, HGRN_DK), F32)
    _, s_meta = _hgrn_scan(qm[None], zm[None], im[None], gm[None], lbp, gg, s0)
    yr, _ = _hgrn_scan(seq3(qr), seq3(zr), seq3(ir), seq3(gr), lbp, gg, s_meta)
    out = _ffn(yr.reshape(bsz * seq, d), w_hgrn_out, h_real, row(ffn_norm[1]), w_ffn_in[1], w_ffn_out[1],
               row(final_norm), True)
    return out.reshape(bsz, seq, d)
```

```python
import functools
import math

import numpy as np
import jax
import jax.numpy as jnp
from jax import lax
from jax.experimental import pallas as pl
from jax.experimental.pallas import tpu as pltpu

D_MODEL = 1024
N_META = 16
FOX_HEADS = 16
FOX_HEAD_DIM = 64
HGRN_HEADS = 8
HGRN_DK = 128
HGRN_CHUNK = 64
FFN_HIDDEN = 2816
EPS = 1e-6

LANES = 128
FOX_PAIRS = FOX_HEADS // 2
FFN_TILE = 256
NEG_BIG = -1e30
PAD_KEY_BIAS = -30000.0
LOG2E = math.log2(math.e)
BOUND_LIMIT = 56.0
SKIP_BITS = 152.0
FAST_RANGE = 56.0
VMEM_LIMIT = 52 * 1024 * 1024

ROW_TILE = 512
ATTN_TILE = 512
SCAN_TILE = 256

AUG_C = 0
AUG_ONE = 3
AUG_BOUND = 6
C3_MID, C3_LO, C3_ONE = 16, 32, 48

F32 = jnp.float32
BF16 = jnp.bfloat16

NN = (((1,), (0,)), ((), ()))
NT = (((1,), (1,)), ((), ()))
TN = (((0,), (0,)), ((), ()))


def _dot(a, b, dims=NN):
    return lax.dot_general(a, b, dims, preferred_element_type=F32)


def _params(*sem):
    return pltpu.CompilerParams(dimension_semantics=sem, vmem_limit_bytes=VMEM_LIMIT)


def _row_tile(m, pref):
    return pref if m % pref == 0 else m


def _resident(shape):
    return pl.BlockSpec(shape, lambda *_: (0,) * len(shape), pipeline_mode=pl.Buffered(1))


def _sigmoid(x):
    return 1.0 / (1.0 + jnp.exp(-x))


def _rms_rows(x, gain):
    ms = jnp.mean(x * x, axis=-1, keepdims=True)
    return x * lax.rsqrt(ms + EPS) * gain


def _split3(x):
    hi = x.astype(BF16)
    r1 = x - hi.astype(F32)
    mid = r1.astype(BF16)
    return hi, mid, (r1 - mid.astype(F32)).astype(BF16)


def _tri_cumsum(tri, x, pieces):
    if pieces == 3:
        hi, mid, lo = _split3(x)
        return _dot(tri, hi) + _dot(tri, mid) + _dot(tri, lo)
    hi = x.astype(BF16)
    return _dot(tri, hi) + _dot(tri, (x - hi.astype(F32)).astype(BF16))


def _block_tril(t, c):
    r = np.arange(t)
    return jnp.asarray((r[None, :] <= r[:, None]) & (r[None, :] // c == r[:, None] // c), BF16)


def _fox_proj_kernel(tiles_per_seq, x_ref, gain_ref, w_ref, wf_ref, bias_ref, c0_ref, tri_ref, bd_ref, ex_ref,
                     qg_ref, kg_ref, pq_ref, pk_ref, q_out, k_out, v_out, g_out, c_end, carry_s):
    @pl.when(lax.rem(pl.program_id(0), tiles_per_seq) == 0)
    def _():
        carry_s[...] = c0_ref[...]

    xn = _rms_rows(x_ref[...], gain_ref[...]).astype(BF16)
    lane = lax.broadcasted_iota(jnp.int32, (1, LANES), 1)
    lo = lane < FOX_HEAD_DIM

    zz = _dot(xn, wf_ref[...]) + bias_ref[...]
    log_f = jnp.minimum(zz, 0.0) - jnp.log1p(jnp.exp(-jnp.abs(zz)))
    c = _tri_cumsum(tri_ref[...], log_f, 3) + carry_s[...]
    tm = c.shape[0]
    carry_s[...] = c[tm - 1:tm, :]
    c_end[0] = c[tm - 1:tm, :]

    cc = jnp.where(lane < FOX_HEADS, c * LOG2E, 0.0)
    hi, mid, low = _split3(cc)
    c3 = (hi.astype(F32) + pltpu.roll(mid.astype(F32), C3_MID, axis=1)
          + pltpu.roll(low.astype(F32), C3_LO, axis=1) + jnp.where(lane == C3_ONE, 1.0, 0.0)).astype(BF16)

    def head_slabs(acc, gain, place_ref, out):
        ss = _dot((acc * acc).astype(BF16), bd_ref[...])
        r = lax.rsqrt(ss * (1.0 / FOX_HEAD_DIM) + EPS)
        r_hi = r.astype(BF16)
        r_lo = (r - r_hi.astype(F32)).astype(BF16)
        xh = acc * _dot(jnp.concatenate([r_hi, r_lo], axis=1), ex_ref[...]) * gain
        aug = _dot(c3, place_ref[...])
        for p in range(FOX_PAIRS):
            pair = slice(p * LANES, (p + 1) * LANES)
            out[2 * p] = jnp.where(lo, xh[:, pair], aug[:, pair]).astype(BF16)
            out[2 * p + 1] = jnp.where(lo, aug[:, pair], xh[:, pair]).astype(BF16)

    d = D_MODEL
    head_slabs(_dot(xn, w_ref[:, 0:d]), qg_ref[...], pq_ref, q_out)
    head_slabs(_dot(xn, w_ref[:, d:2 * d]), kg_ref[...], pk_ref, k_out)
    v_out[...] = _dot(xn, w_ref[:, 2 * d:3 * d]).astype(BF16)
    g_out[...] = _dot(xn, w_ref[:, 3 * d:4 * d])


def _fox_proj(x, seq, gain, w, wf, bias, c0, bd, ex, qg, kg, pq, pk):
    m = x.shape[0]
    tm = _row_tile(seq, ROW_TILE)
    d = D_MODEL
    row = lambda i: (i, 0)
    slab = pl.BlockSpec((FOX_HEADS, tm, LANES), lambda i: (0, i, 0))
    return pl.pallas_call(
        functools.partial(_fox_proj_kernel, seq // tm),
        grid=(m // tm,),
        in_specs=[
            pl.BlockSpec((tm, d), row),
            _resident((1, d)),
            _resident((d, 4 * d)),
            _resident((d, LANES)),
            _resident((1, LANES)),
            _resident((1, LANES)),
            _resident((tm, tm)),
            _resident((d, LANES)),
            _resident((2 * LANES, d)),
            _resident((1, d)),
            _resident((1, d)),
            _resident((LANES, d)),
            _resident((LANES, d)),
        ],
        out_specs=[slab, slab, pl.BlockSpec((tm, d), row), pl.BlockSpec((tm, d), row),
                   pl.BlockSpec((1, 1, LANES), lambda i: (i, 0, 0))],
        out_shape=[
            jax.ShapeDtypeStruct((FOX_HEADS, m, LANES), BF16),
            jax.ShapeDtypeStruct((FOX_HEADS, m, LANES), BF16),
            jax.ShapeDtypeStruct((m, d), BF16),
            jax.ShapeDtypeStruct((m, d), F32),
            jax.ShapeDtypeStruct((m // tm, 1, LANES), F32),
        ],
        scratch_shapes=[pltpu.VMEM((1, LANES), F32)],
        compiler_params=_params("arbitrary"),
        name="fox_proj",
    )(x, gain, w, wf, bias, c0, _block_tril(tm, tm), bd, ex, qg, kg, pq, pk)


def _bound_lane(h):
    return (FOX_HEAD_DIM if h % 2 == 0 else 0) + AUG_BOUND


def _placement(bound2):
    pq = np.zeros((LANES, D_MODEL), np.float32)
    pk = np.zeros((LANES, D_MODEL), np.float32)
    pb = np.zeros((LANES, D_MODEL), np.float32)
    for h in range(FOX_HEADS):
        base = (h // 2) * LANES + (FOX_HEAD_DIM if h % 2 == 0 else 0)
        for piece, src in enumerate((h, C3_MID + h, C3_LO + h)):
            pq[src, base + AUG_C + piece] = 1.0
            pk[src, base + AUG_ONE + piece] = -1.0
            pq[C3_ONE, base + AUG_ONE + piece] = 1.0
            pk[C3_ONE, base + AUG_C + piece] = 1.0
        pq[C3_ONE, base + AUG_BOUND] = 1.0
        pb[C3_ONE, base + AUG_BOUND] = 1.0
    return jnp.asarray(pq, BF16), (jnp.asarray(pk) - jnp.asarray(pb) * bound2).astype(BF16)


def _first_key_tile(c_tile_end, c_start, bsz):
    nt = c_tile_end.shape[0] // bsz
    ce = c_tile_end.reshape(bsz, nt, LANES)[:, :, :FOX_HEADS] * LOG2E
    before = jnp.concatenate([jnp.broadcast_to(c_start[:, :FOX_HEADS] * LOG2E, (bsz, 1, FOX_HEADS)),
                              ce[:, :-1]], axis=1)
    gap = before[:, :, None, :] - ce[:, None, :, :]
    earlier = (jnp.arange(nt)[None, :] < jnp.arange(nt)[:, None])[None, :, :, None]
    dead = jnp.logical_and(gap < -SKIP_BITS, earlier)
    dead = jnp.logical_and(dead[..., 0::2], dead[..., 1::2])
    first = jnp.sum(dead.astype(jnp.int32), axis=2)
    return first.transpose(0, 2, 1).reshape(-1)


def _fox_attn_kernel(has_meta, t, nt, lp, flag_ref, first_ref, q_ref, k_ref, v_ref, gate_ref, *rest):
    if has_meta:
        km_ref, vm_ref, o_ref, m_s, l_s, lp_s, acc_s = rest
    else:
        o_ref, m_s, l_s, lp_s, acc_s = rest
    i = pl.program_id(2)
    first = first_ref[(pl.program_id(0) * FOX_PAIRS + pl.program_id(1)) * nt + i]

    lane = lax.broadcasted_iota(jnp.int32, (1, LANES), 1)
    lo = lane < FOX_HEAD_DIM
    half = (lo, jnp.logical_not(lo))

    def run(online):
        def tile(keys, vt, mask):
            alphas, pv = [], None
            zero_v = jnp.zeros_like(vt)
            for a in range(2):
                s = _dot(q_ref[a, 0], keys(a), NT)
                if mask is not None:
                    s = jnp.where(mask, s, NEG_BIG)
                if online:
                    m_prev = m_s[a]
                    m_new = jnp.maximum(m_prev, jnp.max(s, axis=1, keepdims=True))
                    alpha = jnp.exp2(m_prev - m_new)
                    p = jnp.exp2(s - m_new)
                    l_s[a] = alpha * l_s[a] + jnp.sum(p, axis=1, keepdims=True)
                    m_s[a] = m_new
                    alphas.append(alpha)
                else:
                    p = jnp.exp2(s)
                    part = lp_s[a]
                    for c0 in range(0, p.shape[1], lp):
                        part = part + p[:, c0:c0 + lp]
                    lp_s[a] = part
                d = _dot(p.astype(BF16), jnp.where(half[a], vt, zero_v))
                pv = d if pv is None else pv + d
            if online:
                acc_s[...] = acc_s[...] * jnp.where(lo, alphas[0], alphas[1]) + pv
            else:
                acc_s[...] += pv

        def real_tile(off, size, mask):
            tile(lambda a: k_ref[a, 0, pl.ds(off, size), :], v_ref[0, pl.ds(off, size), :], mask)

        if online:
            m_s[...] = jnp.full(m_s.shape, NEG_BIG, F32)
            l_s[...] = jnp.zeros(l_s.shape, F32)
        else:
            lp_s[...] = jnp.zeros(lp_s.shape, F32)
        acc_s[...] = jnp.zeros(acc_s.shape, F32)

        if nt >= 2:
            def body(j, carry):
                real_tile(pl.multiple_of(j * (2 * t), 2 * t), 2 * t, None)
                return carry

            start = jnp.int32(0) if online else lax.shift_right_logical(first, 1)
            lax.fori_loop(start, lax.shift_right_logical(i, 1), body, 0)

            @pl.when((i & 1) == 1)
            def _():
                real_tile(pl.multiple_of((i - 1) * t, t), t, None)

        off = pl.multiple_of(i * t, t)
        n_pre = LANES if has_meta else 0
        row = lax.broadcasted_iota(jnp.int32, (t, n_pre + t), 0)
        col = lax.broadcasted_iota(jnp.int32, (t, n_pre + t), 1)
        if has_meta:
            tile(lambda a: jnp.concatenate([km_ref[a], k_ref[a, 0, pl.ds(off, t), :]], axis=0),
                 jnp.concatenate([vm_ref[...], v_ref[0, pl.ds(off, t), :]], axis=0),
                 col - n_pre <= row)
        else:
            real_tile(off, t, col <= row)

        if online:
            inv = jnp.where(lo, 1.0 / l_s[0], 1.0 / l_s[1])
        else:
            inv = jnp.where(lo, 1.0 / jnp.sum(lp_s[0], axis=1, keepdims=True),
                            1.0 / jnp.sum(lp_s[1], axis=1, keepdims=True))
        o_ref[0] = (acc_s[...] * inv * _sigmoid(gate_ref[0])).astype(BF16)

    @pl.when(flag_ref[0] == 1)
    def _():
        run(False)

    @pl.when(flag_ref[0] != 1)
    def _():
        run(True)


def _fox_attn(flag, first, q, k, v, gate, meta=None):
    b, l, d = v.shape
    t = _row_tile(l, ATTN_TILE)
    nt = l // t
    lp = LANES if t % LANES == 0 else t
    in_specs = [
        pl.BlockSpec(memory_space=pltpu.SMEM),
        pl.BlockSpec(memory_space=pltpu.SMEM),
        pl.BlockSpec((2, 1, t, LANES), lambda bi, hp, i: (hp, bi, i, 0)),
        pl.BlockSpec((2, 1, l, LANES), lambda bi, hp, i: (hp, bi, 0, 0)),
        pl.BlockSpec((1, l, LANES), lambda bi, hp, i: (bi, 0, hp)),
        pl.BlockSpec((1, t, LANES), lambda bi, hp, i: (bi, i, hp)),
    ]
    args = [flag, first, q, k, v, gate]
    if meta is not None:
        km, vm = meta
        in_specs += [
            pl.BlockSpec((2, LANES, LANES), lambda bi, hp, i: (hp, 0, 0)),
            pl.BlockSpec((LANES, LANES), lambda bi, hp, i: (0, hp)),
        ]
        args += [km, vm]
    return pl.pallas_call(
        functools.partial(_fox_attn_kernel, meta is not None, t, nt, lp),
        grid=(b, FOX_PAIRS, nt),
        in_specs=in_specs,
        out_specs=pl.BlockSpec((1, t, LANES), lambda bi, hp, i: (bi, i, hp)),
        out_shape=jax.ShapeDtypeStruct((b, l, d), BF16),
        scratch_shapes=[
            pltpu.VMEM((2, t, 1), F32),
            pltpu.VMEM((2, t, 1), F32),
            pltpu.VMEM((2, t, lp), F32),
            pltpu.VMEM((t, LANES), F32),
        ],
        compiler_params=_params("arbitrary", "arbitrary", "arbitrary"),
        name="fox_attn",
    )(*args)


def _ffn_kernel(final, nh, y_ref, wmix_ref, h_ref, gain_ref, win_ref, wout_ref, fgain_ref, o_ref,
                hm_s, xn_s, acc_s):
    hm = h_ref[...] + _dot(y_ref[...], wmix_ref[...])
    hm_s[...] = hm
    xn_s[...] = _rms_rows(hm, gain_ref[...]).astype(BF16)
    acc_s[...] = jnp.zeros(acc_s.shape, F32)

    for j in range(nh):
        xn = xn_s[...]
        g = _dot(xn, win_ref[:, j * FFN_TILE:(j + 1) * FFN_TILE])
        u = _dot(xn, win_ref[:, FFN_HIDDEN + j * FFN_TILE:FFN_HIDDEN + (j + 1) * FFN_TILE])
        acc_s[...] += _dot((g * _sigmoid(g) * u).astype(BF16), wout_ref[j * FFN_TILE:(j + 1) * FFN_TILE, :])

    out = hm_s[...] + acc_s[...]
    if final:
        out = _rms_rows(out, fgain_ref[...])
    o_ref[...] = out


def _ffn(y, w_mix, h, gain, w_in, w_out, fgain, final):
    m, d = h.shape
    tm = _row_tile(m, ROW_TILE)
    nh = FFN_HIDDEN // FFN_TILE
    row = lambda i: (i, 0)
    return pl.pallas_call(
        functools.partial(_ffn_kernel, final, nh),
        grid=(m // tm,),
        in_specs=[
            pl.BlockSpec((tm, d), row),
            _resident((d, d)),
            pl.BlockSpec((tm, d), row),
            _resident((1, d)),
            _resident((d, 2 * FFN_HIDDEN)),
            _resident((FFN_HIDDEN, d)),
            _resident((1, d)),
        ],
        out_specs=pl.BlockSpec((tm, d), row),
        out_shape=jax.ShapeDtypeStruct((m, d), F32),
        scratch_shapes=[pltpu.VMEM((tm, d), F32), pltpu.VMEM((tm, d), BF16), pltpu.VMEM((tm, d), F32)],
        compiler_params=_params("arbitrary"),
        name="ffn",
    )(y, w_mix, h, gain, w_in, w_out, fgain)


def _hgrn_proj_kernel(x_ref, gain_ref, w_ref, q_out, z_out, i_out, g_out):
    d = D_MODEL
    xn = _rms_rows(x_ref[...], gain_ref[...]).astype(BF16)
    q_out[...] = _dot(xn, w_ref[:, 0:d]).astype(BF16)
    z_out[...] = _dot(xn, w_ref[:, d:2 * d])
    i_out[...] = _dot(xn, w_ref[:, 2 * d:3 * d]).astype(BF16)
    g_out[...] = _dot(xn, w_ref[:, 3 * d:4 * d]).astype(BF16)


def _hgrn_proj(x, gain, w):
    m, d = x.shape
    tm = _row_tile(m, ROW_TILE)
    row = lambda i: (i, 0)
    return pl.pallas_call(
        _hgrn_proj_kernel,
        grid=(m // tm,),
        in_specs=[pl.BlockSpec((tm, d), row), _resident((1, d)), _resident((d, 4 * d))],
        out_specs=[pl.BlockSpec((tm, d), row)] * 4,
        out_shape=[jax.ShapeDtypeStruct((m, d), BF16), jax.ShapeDtypeStruct((m, d), F32),
                   jax.ShapeDtypeStruct((m, d), BF16), jax.ShapeDtypeStruct((m, d), BF16)],
        compiler_params=_params("arbitrary"),
        name="hgrn_proj",
    )(x, gain, w)


def _hgrn_scan_kernel(t, c, q_ref, z_ref, i_ref, g_ref, lbp_ref, gg_ref, tri_ref, s0_ref,
                      y_ref, sfin_ref, st_s, oi_s, oa_s, qin_s, kout_s, qf_s, kf_s, b_s, q_s, k_s, v_s):
    step = pl.program_id(1)
    nc = t // c
    mid = c // 2
    heads = [slice(h * HGRN_DK, (h + 1) * HGRN_DK) for h in range(HGRN_HEADS)]

    @pl.when(step == 0)
    def _():
        st_s[...] = s0_ref[...]

    lbp = lbp_ref[...]
    mx = jnp.maximum(lbp[0:1], lbp[1:2])
    e0 = jnp.exp(lbp[0:1] - mx)
    e1 = jnp.exp(lbp[1:2] - mx)
    den = e0 + e1
    sm0 = e0 / den
    lb = (sm0 + e1 / den) - sm0

    sig = _sigmoid(z_ref[0])
    log_f = jnp.log2(lb + (1.0 - lb) * sig)
    kk = (1.0 - lb) * (1.0 - sig)
    qz = q_ref[0].astype(F32)
    qq = qz * _sigmoid(qz)
    v_b = i_ref[0]

    b = _tri_cumsum(tri_ref[...], log_f, 2)

    spread = jnp.float32(0.0)
    decay = []
    for ci in range(nc):
        rs = slice(ci * c, (ci + 1) * c)
        bc = b[rs]
        b_last = bc[c - 1:c]
        b_mid = bc[mid - 1:mid]
        spread = jnp.maximum(spread, jnp.maximum(jnp.max(bc[0:1] - b_mid), jnp.max(b_mid - b_last)))
        q_in = qq[rs] * jnp.exp2(bc)
        k_out = kk[rs] * jnp.exp2(b_last - bc)
        qin_s[rs] = q_in.astype(BF16)
        kout_s[rs] = k_out.astype(BF16)
        qf_s[rs] = (q_in * jnp.exp2(-b_mid)).astype(BF16)
        kf_s[rs] = (k_out * jnp.exp2(b_mid - b_last)).astype(BF16)
        decay.append(jnp.exp2(b_last))

    keep = tri_ref[...] > 0
    for sl in heads:
        a = jnp.where(keep, _dot(qf_s[:, sl], kf_s[:, sl], NT), 0.0)
        oa_s[:, sl] = _dot(a.astype(BF16), v_b[:, sl])

    for ci in range(nc):
        rs = slice(ci * c, (ci + 1) * c)
        for h, sl in enumerate(heads):
            st = st_s[h]
            oi_s[rs, sl] = _dot(qin_s[rs, sl], st.astype(BF16), NT)
            st_s[h] = st * decay[ci][:, sl] + _dot(v_b[rs, sl], kout_s[rs, sl], TN)

    @pl.when(spread > FAST_RANGE)
    def _():
        b_s[...] = b
        k_s[...] = kk
        q_s[...] = qq
        v_s[...] = v_b.astype(F32)
        oa_s[...] = jnp.zeros(oa_s.shape, F32)
        rowc = lax.broadcasted_iota(jnp.int32, (c, 1), 0)

        def body(s, carry):
            base = pl.multiple_of((s // c) * c, c)
            blk = pl.ds(base, c)
            p = (q_s[blk, :] * jnp.exp2(jnp.minimum(b_s[blk, :] - b_s[pl.ds(s, 1), :], 0.0))
                 * k_s[pl.ds(s, 1), :])
            vs = v_s[pl.ds(s, 1), :]
            seen = rowc + base >= s
            for sl in heads:
                w = jnp.where(seen, jnp.sum(p[:, sl], axis=1, keepdims=True), 0.0)
                oa_s[blk, sl] += w * vs[:, sl]
            return carry

        lax.fori_loop(0, t, body, 0)

    gz = g_ref[0].astype(F32)
    gate = gz * _sigmoid(gz)
    o = oi_s[...] + oa_s[...]
    for sl in heads:
        y_ref[0, :, sl] = (_rms_rows(o[:, sl], gg_ref[:, sl]) * gate[:, sl]).astype(BF16)

    @pl.when(step == pl.num_programs(1) - 1)
    def _():
        sfin_ref[...] = st_s[...]


def _hgrn_scan(q, z, i, g, lbp, gg, s0):
    b, l, d = q.shape
    t = _row_tile(l, SCAN_TILE)
    c = _row_tile(t, HGRN_CHUNK)
    blk = lambda bi, s: (bi, s, 0)
    st_shape = (HGRN_HEADS, HGRN_DK, HGRN_DK)
    return pl.pallas_call(
        functools.partial(_hgrn_scan_kernel, t, c),
        grid=(b, l // t),
        in_specs=[
            pl.BlockSpec((1, t, d), blk),
            pl.BlockSpec((1, t, d), blk),
            pl.BlockSpec((1, t, d), blk),
            pl.BlockSpec((1, t, d), blk),
            _resident((2, d)),
            _resident((1, d)),
            _resident((t, t)),
            _resident(st_shape),
        ],
        out_specs=[pl.BlockSpec((1, t, d), blk), pl.BlockSpec(st_shape, lambda bi, s: (0, 0, 0))],
        out_shape=[jax.ShapeDtypeStruct((b, l, d), BF16), jax.ShapeDtypeStruct(st_shape, F32)],
        scratch_shapes=([pltpu.VMEM(st_shape, F32)] + [pltpu.VMEM((t, d), F32)] * 2
                        + [pltpu.VMEM((t, d), BF16)] * 4 + [pltpu.VMEM((t, d), F32)] * 4),
        compiler_params=_params("arbitrary", "arbitrary"),
        name="hgrn_scan",
    )(q, z, i, g, lbp, gg, _block_tril(t, c), s0)


def kernel(x, meta_tokens, attn_norm, ffn_norm, final_norm, fox_w_in, fox_b_f, fox_q_norm, fox_k_norm,
           fox_w_out, hgrn_w_in, hgrn_lower_bounds, hgrn_g_norm, hgrn_w_out, ffn_w_in, ffn_w_out):
    bsz, seq, d = x.shape
    row = lambda v: v.reshape(1, -1).astype(F32)

    w_fox = fox_w_in[0, :, :4 * d].astype(BF16)
    w_f = jnp.pad(fox_w_in[0, :, 4 * d:], ((0, 0), (0, LANES - FOX_HEADS))).astype(BF16)
    w_fox_out = fox_w_out[0].astype(BF16)
    w_hgrn = hgrn_w_in[0].astype(BF16)
    w_hgrn_out = hgrn_w_out[0].astype(BF16)
    w_ffn_in = ffn_w_in.astype(BF16)
    w_ffn_out = ffn_w_out.astype(BF16)
    head_of = np.arange(d) // FOX_HEAD_DIM
    bd = jnp.asarray(head_of[:, None] == np.arange(LANES)[None, :], BF16)
    ex = jnp.concatenate([bd.T, bd.T], axis=0)
    qg = row(jnp.tile(fox_q_norm[0], FOX_HEADS)) * (FOX_HEAD_DIM ** -0.5 * LOG2E)
    kg = row(jnp.tile(fox_k_norm[0], FOX_HEADS))
    bound2 = (1.01 * LOG2E * FOX_HEAD_DIM ** 0.5 * jnp.max(jnp.abs(fox_q_norm[0]))
              * jnp.max(jnp.abs(fox_k_norm[0]))).astype(BF16).astype(F32)
    flag = (bound2 <= BOUND_LIMIT).astype(jnp.int32).reshape(1)
    pq, pk = _placement(bound2)
    b_f = jnp.pad(row(fox_b_f[0]), ((0, 0), (0, LANES - FOX_HEADS)))
    gg = row(jnp.tile(hgrn_g_norm[0], HGRN_HEADS))
    lbp = hgrn_lower_bounds.astype(F32)

    h_meta = meta_tokens.astype(F32)
    h_real = x.reshape(bsz * seq, d)
    seq3 = lambda a: a.reshape(bsz, seq, -1)

    proj = lambda h, n, c0: _fox_proj(h, n, row(attn_norm[0]), w_fox, w_f, b_f, c0, bd, ex, qg, kg, pq, pk)
    qm, km, vm, gm, c_meta = proj(h_meta, N_META, jnp.zeros((1, LANES), F32))
    qr, kr, vr, gr, c_tiles = proj(h_real, seq, c_meta[0])
    ym = _fox_attn(flag, jnp.zeros((FOX_PAIRS,), jnp.int32), qm[:, None], km[:, None], vm[None], gm[None])[0]
    pad_bias = np.zeros((FOX_HEADS, LANES, LANES), np.float32)
    for h in range(FOX_HEADS):
        pad_bias[h, N_META:, _bound_lane(h)] = PAD_KEY_BIAS
    km_pad = jnp.pad(km, ((0, 0), (0, LANES - N_META), (0, 0))) + jnp.asarray(pad_bias, BF16)
    vm_pad = jnp.pad(vm, ((0, LANES - N_META), (0, 0)))
    heads4 = lambda a: a.reshape(FOX_HEADS, bsz, seq, LANES)
    yr = _fox_attn(flag, _first_key_tile(c_tiles, c_meta[0], bsz), heads4(qr), heads4(kr), seq3(vr), seq3(gr),
                   meta=(km_pad, vm_pad)).reshape(bsz * seq, d)
    ffn0 = lambda y, h: _ffn(y, w_fox_out, h, row(ffn_norm[0]), w_ffn_in[0], w_ffn_out[0],
                             row(final_norm), False)
    h_meta = ffn0(ym, h_meta)
    h_real = ffn0(yr, h_real)

    proj = lambda h: _hgrn_proj(h, row(attn_norm[1]), w_hgrn)
    qm, zm, im, gm = proj(h_meta)
    qr, zr, ir, gr = proj(h_real)
    s0 = jnp.zeros((HGRN_HEADS, HGRN_DK, HGRN_DK), F32)
    _, s_meta = _hgrn_scan(qm[None], zm[None], im[None], gm[None], lbp, gg, s0)
    yr, _ = _hgrn_scan(seq3(qr), seq3(zr), seq3(ir), seq3(gr), lbp, gg, s_meta)
    out = _ffn(yr.reshape(bsz * seq, d), w_hgrn_out, h_real, row(ffn_norm[1]), w_ffn_in[1], w_ffn_out[1],
               row(final_norm), True)
    return out.reshape(bsz, seq, d)
```

```python
import functools
import math

import numpy as np
import jax
import jax.numpy as jnp
from jax import lax
from jax.experimental import pallas as pl
from jax.experimental.pallas import tpu as pltpu

D_MODEL = 1024
N_META = 16
FOX_HEADS = 16
FOX_HEAD_DIM = 64
HGRN_HEADS = 8
HGRN_DK = 128
HGRN_CHUNK = 64
FFN_HIDDEN = 2816
EPS = 1e-6

LANES = 128
FOX_PAIRS = FOX_HEADS // 2
FFN_TILE = 256
NEG_BIG = -1e30
PAD_KEY_BIAS = -30000.0
LOG2E = math.log2(math.e)
BOUND_LIMIT = 56.0
SKIP_BITS = 152.0
FAST_RANGE = 56.0
VMEM_LIMIT = 52 * 1024 * 1024

ROW_TILE = 512
ATTN_TILE = 512
SCAN_TILE = 256

AUG_C = 0
AUG_ONE = 3
AUG_BOUND = 6
C3_MID, C3_LO, C3_ONE = 16, 32, 48

F32 = jnp.float32
BF16 = jnp.bfloat16

NN = (((1,), (0,)), ((), ()))
NT = (((1,), (1,)), ((), ()))
TN = (((0,), (0,)), ((), ()))


def _dot(a, b, dims=NN):
    return lax.dot_general(a, b, dims, preferred_element_type=F32)


def _params(*sem):
    return pltpu.CompilerParams(dimension_semantics=sem, vmem_limit_bytes=VMEM_LIMIT)


def _row_tile(m, pref):
    return pref if m % pref == 0 else m


def _resident(shape):
    return pl.BlockSpec(shape, lambda *_: (0,) * len(shape), pipeline_mode=pl.Buffered(1))


def _sigmoid(x):
    return 1.0 / (1.0 + jnp.exp(-x))


def _rms_rows(x, gain):
    ms = jnp.mean(x * x, axis=-1, keepdims=True)
    return x * lax.rsqrt(ms + EPS) * gain


def _split3(x):
    hi = x.astype(BF16)
    r1 = x - hi.astype(F32)
    mid = r1.astype(BF16)
    return hi, mid, (r1 - mid.astype(F32)).astype(BF16)


def _tri_cumsum(tri, x, pieces):
    if pieces == 3:
        hi, mid, lo = _split3(x)
        return _dot(tri, hi) + _dot(tri, mid) + _dot(tri, lo)
    hi = x.astype(BF16)
    return _dot(tri, hi) + _dot(tri, (x - hi.astype(F32)).astype(BF16))


def _block_tril(t, c):
    r = np.arange(t)
    return jnp.asarray((r[None, :] <= r[:, None]) & (r[None, :] // c == r[:, None] // c), BF16)


def _fox_proj_kernel(tiles_per_seq, x_ref, gain_ref, w_ref, wf_ref, bias_ref, c0_ref, tri_ref, bd_ref, ex_ref,
                     qg_ref, kg_ref, pq_ref, pk_ref, q_out, k_out, v_out, g_out, c_end, carry_s):
    @pl.when(lax.rem(pl.program_id(0), tiles_per_seq) == 0)
    def _():
        carry_s[...] = c0_ref[...]

    xn = _rms_rows(x_ref[...], gain_ref[...]).astype(BF16)
    lane = lax.broadcasted_iota(jnp.int32, (1, LANES), 1)
    lo = lane < FOX_HEAD_DIM

    zz = _dot(xn, wf_ref[...]) + bias_ref[...]
    log_f = jnp.minimum(zz, 0.0) - jnp.log1p(jnp.exp(-jnp.abs(zz)))
    c = _tri_cumsum(tri_ref[...], log_f, 3) + carry_s[...]
    tm = c.shape[0]
    carry_s[...] = c[tm - 1:tm, :]
    c_end[0] = c[tm - 1:tm, :]

    cc = jnp.where(lane < FOX_HEADS, c * LOG2E, 0.0)
    hi, mid, low = _split3(cc)
    c3 = (hi.astype(F32) + pltpu.roll(mid.astype(F32), C3_MID, axis=1)
          + pltpu.roll(low.astype(F32), C3_LO, axis=1) + jnp.where(lane == C3_ONE, 1.0, 0.0)).astype(BF16)

    def head_slabs(acc, gain, place_ref, out):
        ss = _dot((acc * acc).astype(BF16), bd_ref[...])
        r = lax.rsqrt(ss * (1.0 / FOX_HEAD_DIM) + EPS)
        r_hi = r.astype(BF16)
        r_lo = (r - r_hi.astype(F32)).astype(BF16)
        xh = acc * _dot(jnp.concatenate([r_hi, r_lo], axis=1), ex_ref[...]) * gain
        aug = _dot(c3, place_ref[...])
        for p in range(FOX_PAIRS):
            pair = slice(p * LANES, (p + 1) * LANES)
            out[2 * p] = jnp.where(lo, xh[:, pair], aug[:, pair]).astype(BF16)
            out[2 * p + 1] = jnp.where(lo, aug[:, pair], xh[:, pair]).astype(BF16)

    d = D_MODEL
    head_slabs(_dot(xn, w_ref[:, 0:d]), qg_ref[...], pq_ref, q_out)
    head_slabs(_dot(xn, w_ref[:, d:2 * d]), kg_ref[...], pk_ref, k_out)
    v_out[...] = _dot(xn, w_ref[:, 2 * d:3 * d]).astype(BF16)
    g_out[...] = _dot(xn, w_ref[:, 3 * d:4 * d])


def _fox_proj(x, seq, gain, w, wf, bias, c0, bd, ex, qg, kg, pq, pk):
    m = x.shape[0]
    tm = _row_tile(seq, ROW_TILE)
    d = D_MODEL
    row = lambda i: (i, 0)
    slab = pl.BlockSpec((FOX_HEADS, tm, LANES), lambda i: (0, i, 0))
    return pl.pallas_call(
        functools.partial(_fox_proj_kernel, seq // tm),
        grid=(m // tm,),
        in_specs=[
            pl.BlockSpec((tm, d), row),
            _resident((1, d)),
            _resident((d, 4 * d)),
            _resident((d, LANES)),
            _resident((1, LANES)),
            _resident((1, LANES)),
            _resident((tm, tm)),
            _resident((d, LANES)),
            _resident((2 * LANES, d)),
            _resident((1, d)),
            _resident((1, d)),
            _resident((LANES, d)),
            _resident((LANES, d)),
        ],
        out_specs=[slab, slab, pl.BlockSpec((tm, d), row), pl.BlockSpec((tm, d), row),
                   pl.BlockSpec((1, 1, LANES), lambda i: (i, 0, 0))],
        out_shape=[
            jax.ShapeDtypeStruct((FOX_HEADS, m, LANES), BF16),
            jax.ShapeDtypeStruct((FOX_HEADS, m, LANES), BF16),
            jax.ShapeDtypeStruct((m, d), BF16),
            jax.ShapeDtypeStruct((m, d), F32),
            jax.ShapeDtypeStruct((m // tm, 1, LANES), F32),
        ],
        scratch_shapes=[pltpu.VMEM((1, LANES), F32)],
        compiler_params=_params("arbitrary"),
        name="fox_proj",
    )(x, gain, w, wf, bias, c0, _block_tril(tm, tm), bd, ex, qg, kg, pq, pk)


def _bound_lane(h):
    return (FOX_HEAD_DIM if h % 2 == 0 else 0) + AUG_BOUND


def _placement(bound2):
    pq = np.zeros((LANES, D_MODEL), np.float32)
    pk = np.zeros((LANES, D_MODEL), np.float32)
    pb = np.zeros((LANES, D_MODEL), np.float32)
    for h in range(FOX_HEADS):
        base = (h // 2) * LANES + (FOX_HEAD_DIM if h % 2 == 0 else 0)
        for piece, src in enumerate((h, C3_MID + h, C3_LO + h)):
            pq[src, base + AUG_C + piece] = 1.0
            pk[src, base + AUG_ONE + piece] = -1.0
            pq[C3_ONE, base + AUG_ONE + piece] = 1.0
            pk[C3_ONE, base + AUG_C + piece] = 1.0
        pq[C3_ONE, base + AUG_BOUND] = 1.0
        pb[C3_ONE, base + AUG_BOUND] = 1.0
    return jnp.asarray(pq, BF16), (jnp.asarray(pk) - jnp.asarray(pb) * bound2).astype(BF16)


def _first_key_tile(c_tile_end, c_start, bsz):
    nt = c_tile_end.shape[0] // bsz
    ce = c_tile_end.reshape(bsz, nt, LANES)[:, :, :FOX_HEADS] * LOG2E
    before = jnp.concatenate([jnp.broadcast_to(c_start[:, :FOX_HEADS] * LOG2E, (bsz, 1, FOX_HEADS)),
                              ce[:, :-1]], axis=1)
    gap = before[:, :, None, :] - ce[:, None, :, :]
    earlier = (jnp.arange(nt)[None, :] < jnp.arange(nt)[:, None])[None, :, :, None]
    dead = jnp.logical_and(gap < -SKIP_BITS, earlier)
    dead = jnp.logical_and(dead[..., 0::2], dead[..., 1::2])
    first = jnp.sum(dead.astype(jnp.int32), axis=2)
    return first.transpose(0, 2, 1).reshape(-1)


def _fox_attn_kernel(has_meta, t, nt, lp, flag_ref, first_ref, q_ref, k_ref, v_ref, gate_ref, *rest):
    if has_meta:
        km_ref, vm_ref, o_ref, m_s, l_s, lp_s, acc_s = rest
    else:
        o_ref, m_s, l_s, lp_s, acc_s = rest
    i = pl.program_id(2)
    first = first_ref[(pl.program_id(0) * FOX_PAIRS + pl.program_id(1)) * nt + i]

    lane = lax.broadcasted_iota(jnp.int32, (1, LANES), 1)
    lo = lane < FOX_HEAD_DIM
    half = (lo, jnp.logical_not(lo))

    def run(online):
        def tile(keys, vt, mask):
            alphas, pv = [], None
            zero_v = jnp.zeros_like(vt)
            for a in range(2):
                s = _dot(q_ref[a, 0], keys(a), NT)
                if mask is not None:
                    s = jnp.where(mask, s, NEG_BIG)
                if online:
                    m_prev = m_s[a]
                    m_new = jnp.maximum(m_prev, jnp.max(s, axis=1, keepdims=True))
                    alpha = jnp.exp2(m_prev - m_new)
                    p = jnp.exp2(s - m_new)
                    l_s[a] = alpha * l_s[a] + jnp.sum(p, axis=1, keepdims=True)
                    m_s[a] = m_new
                    alphas.append(alpha)
                else:
                    p = jnp.exp2(s)
                    part = lp_s[a]
                    for c0 in range(0, p.shape[1], lp):
                        part = part + p[:, c0:c0 + lp]
                    lp_s[a] = part
                d = _dot(p.astype(BF16), jnp.where(half[a], vt, zero_v))
                pv = d if pv is None else pv + d
            if online:
                acc_s[...] = acc_s[...] * jnp.where(lo, alphas[0], alphas[1]) + pv
            else:
                acc_s[...] += pv

        def real_tile(off, size, mask):
            tile(lambda a: k_ref[a, 0, pl.ds(off, size), :], v_ref[0, pl.ds(off, size), :], mask)

        if online:
            m_s[...] = jnp.full(m_s.shape, NEG_BIG, F32)
            l_s[...] = jnp.zeros(l_s.shape, F32)
        else:
            lp_s[...] = jnp.zeros(lp_s.shape, F32)
        acc_s[...] = jnp.zeros(acc_s.shape, F32)

        if nt >= 2:
            def body(j, carry):
                real_tile(pl.multiple_of(j * (2 * t), 2 * t), 2 * t, None)
                return carry

            start = jnp.int32(0) if online else lax.shift_right_logical(first, 1)
            lax.fori_loop(start, lax.shift_right_logical(i, 1), body, 0)

            odd = (i & 1) == 1

            @pl.when(odd if online else jnp.logical_and(odd, first < i))
            def _():
                real_tile(pl.multiple_of((i - 1) * t, t), t, None)

        off = pl.multiple_of(i * t, t)
        n_pre = LANES if has_meta else 0
        row = lax.broadcasted_iota(jnp.int32, (t, n_pre + t), 0)
        col = lax.broadcasted_iota(jnp.int32, (t, n_pre + t), 1)
        if has_meta:
            tile(lambda a: jnp.concatenate([km_ref[a], k_ref[a, 0, pl.ds(off, t), :]], axis=0),
                 jnp.concatenate([vm_ref[...], v_ref[0, pl.ds(off, t), :]], axis=0),
                 col - n_pre <= row)
        else:
            real_tile(off, t, col <= row)

        if online:
            inv = jnp.where(lo, 1.0 / l_s[0], 1.0 / l_s[1])
        else:
            inv = jnp.where(lo, 1.0 / jnp.sum(lp_s[0], axis=1, keepdims=True),
                            1.0 / jnp.sum(lp_s[1], axis=1, keepdims=True))
        o_ref[0] = (acc_s[...] * inv * _sigmoid(gate_ref[0])).astype(BF16)

    @pl.when(flag_ref[0] == 1)
    def _():
        run(False)

    @pl.when(flag_ref[0] != 1)
    def _():
        run(True)


def _fox_attn(flag, first, q, k, v, gate, meta=None):
    b, l, d = v.shape
    t = _row_tile(l, ATTN_TILE)
    nt = l // t
    lp = LANES if t % LANES == 0 else t
    in_specs = [
        pl.BlockSpec(memory_space=pltpu.SMEM),
        pl.BlockSpec(memory_space=pltpu.SMEM),
        pl.BlockSpec((2, 1, t, LANES), lambda bi, hp, i: (hp, bi, i, 0)),
        pl.BlockSpec((2, 1, l, LANES), lambda bi, hp, i: (hp, bi, 0, 0)),
        pl.BlockSpec((1, l, LANES), lambda bi, hp, i: (bi, 0, hp)),
        pl.BlockSpec((1, t, LANES), lambda bi, hp, i: (bi, i, hp)),
    ]
    args = [flag, first, q, k, v, gate]
    if meta is not None:
        km, vm = meta
        in_specs += [
            pl.BlockSpec((2, LANES, LANES), lambda bi, hp, i: (hp, 0, 0)),
            pl.BlockSpec((LANES, LANES), lambda bi, hp, i: (0, hp)),
        ]
        args += [km, vm]
    return pl.pallas_call(
        functools.partial(_fox_attn_kernel, meta is not None, t, nt, lp),
        grid=(b, FOX_PAIRS, nt),
        in_specs=in_specs,
        out_specs=pl.BlockSpec((1, t, LANES), lambda bi, hp, i: (bi, i, hp)),
        out_shape=jax.ShapeDtypeStruct((b, l, d), BF16),
        scratch_shapes=[
            pltpu.VMEM((2, t, 1), F32),
            pltpu.VMEM((2, t, 1), F32),
            pltpu.VMEM((2, t, lp), F32),
            pltpu.VMEM((t, LANES), F32),
        ],
        compiler_params=_params("arbitrary", "arbitrary", "arbitrary"),
        name="fox_attn",
    )(*args)


def _ffn_kernel(final, nh, y_ref, wmix_ref, h_ref, gain_ref, win_ref, wout_ref, fgain_ref, o_ref,
                hm_s, xn_s, acc_s):
    hm = h_ref[...] + _dot(y_ref[...], wmix_ref[...])
    hm_s[...] = hm
    xn_s[...] = _rms_rows(hm, gain_ref[...]).astype(BF16)
    acc_s[...] = jnp.zeros(acc_s.shape, F32)

    for j in range(nh):
        xn = xn_s[...]
        g = _dot(xn, win_ref[:, j * FFN_TILE:(j + 1) * FFN_TILE])
        u = _dot(xn, win_ref[:, FFN_HIDDEN + j * FFN_TILE:FFN_HIDDEN + (j + 1) * FFN_TILE])
        acc_s[...] += _dot((g * _sigmoid(g) * u).astype(BF16), wout_ref[j * FFN_TILE:(j + 1) * FFN_TILE, :])

    out = hm_s[...] + acc_s[...]
    if final:
        out = _rms_rows(out, fgain_ref[...])
    o_ref[...] = out


def _ffn(y, w_mix, h, gain, w_in, w_out, fgain, final):
    m, d = h.shape
    tm = _row_tile(m, ROW_TILE)
    nh = FFN_HIDDEN // FFN_TILE
    row = lambda i: (i, 0)
    return pl.pallas_call(
        functools.partial(_ffn_kernel, final, nh),
        grid=(m // tm,),
        in_specs=[
            pl.BlockSpec((tm, d), row),
            _resident((d, d)),
            pl.BlockSpec((tm, d), row),
            _resident((1, d)),
            _resident((d, 2 * FFN_HIDDEN)),
            _resident((FFN_HIDDEN, d)),
            _resident((1, d)),
        ],
        out_specs=pl.BlockSpec((tm, d), row),
        out_shape=jax.ShapeDtypeStruct((m, d), F32),
        scratch_shapes=[pltpu.VMEM((tm, d), F32), pltpu.VMEM((tm, d), BF16), pltpu.VMEM((tm, d), F32)],
        compiler_params=_params("arbitrary"),
        name="ffn",
    )(y, w_mix, h, gain, w_in, w_out, fgain)


def _hgrn_proj_kernel(x_ref, gain_ref, w_ref, q_out, z_out, i_out, g_out):
    d = D_MODEL
    xn = _rms_rows(x_ref[...], gain_ref[...]).astype(BF16)
    q_out[...] = _dot(xn, w_ref[:, 0:d]).astype(BF16)
    z_out[...] = _dot(xn, w_ref[:, d:2 * d])
    i_out[...] = _dot(xn, w_ref[:, 2 * d:3 * d]).astype(BF16)
    g_out[...] = _dot(xn, w_ref[:, 3 * d:4 * d]).astype(BF16)


def _hgrn_proj(x, gain, w):
    m, d = x.shape
    tm = _row_tile(m, ROW_TILE)
    row = lambda i: (i, 0)
    return pl.pallas_call(
        _hgrn_proj_kernel,
        grid=(m // tm,),
        in_specs=[pl.BlockSpec((tm, d), row), _resident((1, d)), _resident((d, 4 * d))],
        out_specs=[pl.BlockSpec((tm, d), row)] * 4,
        out_shape=[jax.ShapeDtypeStruct((m, d), BF16), jax.ShapeDtypeStruct((m, d), F32),
                   jax.ShapeDtypeStruct((m, d), BF16), jax.ShapeDtypeStruct((m, d), BF16)],
        compiler_params=_params("arbitrary"),
        name="hgrn_proj",
    )(x, gain, w)


def _hgrn_scan_kernel(t, c, q_ref, z_ref, i_ref, g_ref, lbp_ref, gg_ref, tri_ref, s0_ref,
                      y_ref, sfin_ref, st_s, oi_s, oa_s, qin_s, kout_s, qf_s, kf_s, b_s, q_s, k_s, v_s):
    step = pl.program_id(1)
    nc = t // c
    mid = c // 2
    heads = [slice(h * HGRN_DK, (h + 1) * HGRN_DK) for h in range(HGRN_HEADS)]

    @pl.when(step == 0)
    def _():
        st_s[...] = s0_ref[...]

    lbp = lbp_ref[...]
    mx = jnp.maximum(lbp[0:1], lbp[1:2])
    e0 = jnp.exp(lbp[0:1] - mx)
    e1 = jnp.exp(lbp[1:2] - mx)
    den = e0 + e1
    sm0 = e0 / den
    lb = (sm0 + e1 / den) - sm0

    sig = _sigmoid(z_ref[0])
    log_f = jnp.log2(lb + (1.0 - lb) * sig)
    kk = (1.0 - lb) * (1.0 - sig)
    qz = q_ref[0].astype(F32)
    qq = qz * _sigmoid(qz)
    v_b = i_ref[0]

    b = _tri_cumsum(tri_ref[...], log_f, 2)

    spread = jnp.float32(0.0)
    decay = []
    for ci in range(nc):
        rs = slice(ci * c, (ci + 1) * c)
        bc = b[rs]
        b_last = bc[c - 1:c]
        b_mid = bc[mid - 1:mid]
        spread = jnp.maximum(spread, jnp.maximum(jnp.max(bc[0:1] - b_mid), jnp.max(b_mid - b_last)))
        q_in = qq[rs] * jnp.exp2(bc)
        k_out = kk[rs] * jnp.exp2(b_last - bc)
        qin_s[rs] = q_in.astype(BF16)
        kout_s[rs] = k_out.astype(BF16)
        qf_s[rs] = (q_in * jnp.exp2(-b_mid)).astype(BF16)
        kf_s[rs] = (k_out * jnp.exp2(b_mid - b_last)).astype(BF16)
        decay.append(jnp.exp2(b_last))

    keep = tri_ref[...] > 0
    for sl in heads:
        a = jnp.where(keep, _dot(qf_s[:, sl], kf_s[:, sl], NT), 0.0)
        oa_s[:, sl] = _dot(a.astype(BF16), v_b[:, sl])

    for ci in range(nc):
        rs = slice(ci * c, (ci + 1) * c)
        for h, sl in enumerate(heads):
            st = st_s[h]
            oi_s[rs, sl] = _dot(qin_s[rs, sl], st.astype(BF16), NT)
            st_s[h] = st * decay[ci][:, sl] + _dot(v_b[rs, sl], kout_s[rs, sl], TN)

    @pl.when(spread > FAST_RANGE)
    def _():
        b_s[...] = b
        k_s[...] = kk
        q_s[...] = qq
        v_s[...] = v_b.astype(F32)
        oa_s[...] = jnp.zeros(oa_s.shape, F32)
        rowc = lax.broadcasted_iota(jnp.int32, (c, 1), 0)

        def body(s, carry):
            base = pl.multiple_of((s // c) * c, c)
            blk = pl.ds(base, c)
            p = (q_s[blk, :] * jnp.exp2(jnp.minimum(b_s[blk, :] - b_s[pl.ds(s, 1), :], 0.0))
                 * k_s[pl.ds(s, 1), :])
            vs = v_s[pl.ds(s, 1), :]
            seen = rowc + base >= s
            for sl in heads:
                w = jnp.where(seen, jnp.sum(p[:, sl], axis=1, keepdims=True), 0.0)
                oa_s[blk, sl] += w * vs[:, sl]
            return carry

        lax.fori_loop(0, t, body, 0)

    gz = g_ref[0].astype(F32)
    gate = gz * _sigmoid(gz)
    o = oi_s[...] + oa_s[...]
    for sl in heads:
        y_ref[0, :, sl] = (_rms_rows(o[:, sl], gg_ref[:, sl]) * gate[:, sl]).astype(BF16)

    @pl.when(step == pl.num_programs(1) - 1)
    def _():
        sfin_ref[...] = st_s[...]


def _hgrn_scan(q, z, i, g, lbp, gg, s0):
    b, l, d = q.shape
    t = _row_tile(l, SCAN_TILE)
    c = _row_tile(t, HGRN_CHUNK)
    blk = lambda bi, s: (bi, s, 0)
    st_shape = (HGRN_HEADS, HGRN_DK, HGRN_DK)
    return pl.pallas_call(
        functools.partial(_hgrn_scan_kernel, t, c),
        grid=(b, l // t),
        in_specs=[
            pl.BlockSpec((1, t, d), blk),
            pl.BlockSpec((1, t, d), blk),
            pl.BlockSpec((1, t, d), blk),
            pl.BlockSpec((1, t, d), blk),
            _resident((2, d)),
            _resident((1, d)),
            _resident((t, t)),
            _resident(st_shape),
        ],
        out_specs=[pl.BlockSpec((1, t, d), blk), pl.BlockSpec(st_shape, lambda bi, s: (0, 0, 0))],
        out_shape=[jax.ShapeDtypeStruct((b, l, d), BF16), jax.ShapeDtypeStruct(st_shape, F32)],
        scratch_shapes=([pltpu.VMEM(st_shape, F32)] + [pltpu.VMEM((t, d), F32)] * 2
                        + [pltpu.VMEM((t, d), BF16)] * 4 + [pltpu.VMEM((t, d), F32)] * 4),
        compiler_params=_params("arbitrary", "arbitrary"),
        name="hgrn_scan",
    )(q, z, i, g, lbp, gg, _block_tril(t, c), s0)


def kernel(x, meta_tokens, attn_norm, ffn_norm, final_norm, fox_w_in, fox_b_f, fox_q_norm, fox_k_norm,
           fox_w_out, hgrn_w_in, hgrn_lower_bounds, hgrn_g_norm, hgrn_w_out, ffn_w_in, ffn_w_out):
    bsz, seq, d = x.shape
    row = lambda v: v.reshape(1, -1).astype(F32)

    order = jnp.argsort(fox_b_f[0])
    cols = (order[:, None] * FOX_HEAD_DIM + jnp.arange(FOX_HEAD_DIM)[None, :]).reshape(-1)
    fox_b_sorted = fox_b_f[0][order]

    w_fox = fox_w_in[0, :, :4 * d].reshape(d, 4, d)[:, :, cols].reshape(d, 4 * d).astype(BF16)
    w_f = jnp.pad(fox_w_in[0, :, 4 * d:][:, order], ((0, 0), (0, LANES - FOX_HEADS))).astype(BF16)
    w_fox_out = fox_w_out[0][cols, :].astype(BF16)
    w_hgrn = hgrn_w_in[0].astype(BF16)
    w_hgrn_out = hgrn_w_out[0].astype(BF16)
    w_ffn_in = ffn_w_in.astype(BF16)
    w_ffn_out = ffn_w_out.astype(BF16)
    head_of = np.arange(d) // FOX_HEAD_DIM
    bd = jnp.asarray(head_of[:, None] == np.arange(LANES)[None, :], BF16)
    ex = jnp.concatenate([bd.T, bd.T], axis=0)
    qg = row(jnp.tile(fox_q_norm[0], FOX_HEADS)) * (FOX_HEAD_DIM ** -0.5 * LOG2E)
    kg = row(jnp.tile(fox_k_norm[0], FOX_HEADS))
    bound2 = (1.01 * LOG2E * FOX_HEAD_DIM ** 0.5 * jnp.max(jnp.abs(fox_q_norm[0]))
              * jnp.max(jnp.abs(fox_k_norm[0]))).astype(BF16).astype(F32)
    flag = (bound2 <= BOUND_LIMIT).astype(jnp.int32).reshape(1)
    pq, pk = _placement(bound2)
    b_f = jnp.pad(row(fox_b_sorted), ((0, 0), (0, LANES - FOX_HEADS)))
    gg = row(jnp.tile(hgrn_g_norm[0], HGRN_HEADS))
    lbp = hgrn_lower_bounds.astype(F32)

    h_meta = meta_tokens.astype(F32)
    h_real = x.reshape(bsz * seq, d)
    seq3 = lambda a: a.reshape(bsz, seq, -1)

    proj = lambda h, n, c0: _fox_proj(h, n, row(attn_norm[0]), w_fox, w_f, b_f, c0, bd, ex, qg, kg, pq, pk)
    qm, km, vm, gm, c_meta = proj(h_meta, N_META, jnp.zeros((1, LANES), F32))
    qr, kr, vr, gr, c_tiles = proj(h_real, seq, c_meta[0])
    ym = _fox_attn(flag, jnp.zeros((FOX_PAIRS,), jnp.int32), qm[:, None], km[:, None], vm[None], gm[None])[0]
    pad_bias = np.zeros((FOX_HEADS, LANES, LANES), np.float32)
    for h in range(FOX_HEADS):
        pad_bias[h, N_META:, _bound_lane(h)] = PAD_KEY_BIAS
    km_pad = jnp.pad(km, ((0, 0), (0, LANES - N_META), (0, 0))) + jnp.asarray(pad_bias, BF16)
    vm_pad = jnp.pad(vm, ((0, LANES - N_META), (0, 0)))
    heads4 = lambda a: a.reshape(FOX_HEADS, bsz, seq, LANES)
    yr = _fox_attn(flag, _first_key_tile(c_tiles, c_meta[0], bsz), heads4(qr), heads4(kr), seq3(vr), seq3(gr),
                   meta=(km_pad, vm_pad)).reshape(bsz * seq, d)
    ffn0 = lambda y, h: _ffn(y, w_fox_out, h, row(ffn_norm[0]), w_ffn_in[0], w_ffn_out[0],
                             row(final_norm), False)
    h_meta = ffn0(ym, h_meta)
    h_real = ffn0(yr, h_real)

    proj = lambda h: _hgrn_proj(h, row(attn_norm[1]), w_hgrn)
    qm, zm, im, gm = proj(h_meta)
    qr, zr, ir, gr = proj(h_real)
    s0 = jnp.zeros((HGRN_HEADS, HGRN_DK, HGRN_DK), F32)
    _, s_meta = _hgrn_scan(qm[None], zm[None], im[None], gm[None], lbp, gg, s0)
    yr, _ = _hgrn_scan(seq3(qr), seq3(zr), seq3(ir), seq3(gr), lbp, gg, s_meta)
    out = _ffn(yr.reshape(bsz * seq, d), w_hgrn_out, h_real, row(ffn_norm[1]), w_ffn_in[1], w_ffn_out[1],
               row(final_norm), True)
    return out.reshape(bsz, seq, d)
```

```python
import functools
import math

import numpy as np
import jax
import jax.numpy as jnp
from jax import lax
from jax.experimental import pallas as pl
from jax.experimental.pallas import tpu as pltpu

D_MODEL = 1024
N_META = 16
FOX_HEADS = 16
FOX_HEAD_DIM = 64
HGRN_HEADS = 8
HGRN_DK = 128
HGRN_CHUNK = 64
FFN_HIDDEN = 2816
EPS = 1e-6

LANES = 128
FOX_PAIRS = FOX_HEADS // 2
FFN_TILE = 256
NEG_BIG = -1e30
PAD_KEY_BIAS = -30000.0
LOG2E = math.log2(math.e)
BOUND_LIMIT = 56.0
SKIP_BITS = 152.0
FAST_RANGE = 56.0
VMEM_LIMIT = 52 * 1024 * 1024

ROW_TILE = 512
ATTN_TILE = 512
ATTN_SUBTILES = 2
SCAN_TILE = 256

AUG_C = 0
AUG_ONE = 3
AUG_BOUND = 6
C3_MID, C3_LO, C3_ONE = 16, 32, 48

F32 = jnp.float32
BF16 = jnp.bfloat16

NN = (((1,), (0,)), ((), ()))
NT = (((1,), (1,)), ((), ()))
TN = (((0,), (0,)), ((), ()))


def _dot(a, b, dims=NN):
    return lax.dot_general(a, b, dims, preferred_element_type=F32)


def _params(*sem):
    return pltpu.CompilerParams(dimension_semantics=sem, vmem_limit_bytes=VMEM_LIMIT)


def _row_tile(m, pref):
    return pref if m % pref == 0 else m


def _resident(shape):
    return pl.BlockSpec(shape, lambda *_: (0,) * len(shape), pipeline_mode=pl.Buffered(1))


def _sigmoid(x):
    return 1.0 / (1.0 + jnp.exp(-x))


def _rms_rows(x, gain):
    ms = jnp.mean(x * x, axis=-1, keepdims=True)
    return x * lax.rsqrt(ms + EPS) * gain


def _split3(x):
    hi = x.astype(BF16)
    r1 = x - hi.astype(F32)
    mid = r1.astype(BF16)
    return hi, mid, (r1 - mid.astype(F32)).astype(BF16)


def _tri_cumsum(tri, x, pieces):
    if pieces == 3:
        hi, mid, lo = _split3(x)
        return _dot(tri, hi) + _dot(tri, mid) + _dot(tri, lo)
    hi = x.astype(BF16)
    return _dot(tri, hi) + _dot(tri, (x - hi.astype(F32)).astype(BF16))


def _block_tril(t, c):
    r = np.arange(t)
    return jnp.asarray((r[None, :] <= r[:, None]) & (r[None, :] // c == r[:, None] // c), BF16)


def _fox_proj_kernel(tiles_per_seq, x_ref, gain_ref, w_ref, wf_ref, bias_ref, c0_ref, tri_ref, bd_ref, ex_ref,
                     qg_ref, kg_ref, pq_ref, pk_ref, q_out, k_out, v_out, g_out, c_end, carry_s):
    @pl.when(lax.rem(pl.program_id(0), tiles_per_seq) == 0)
    def _():
        carry_s[...] = c0_ref[...]

    xn = _rms_rows(x_ref[...], gain_ref[...]).astype(BF16)
    lane = lax.broadcasted_iota(jnp.int32, (1, LANES), 1)
    lo = lane < FOX_HEAD_DIM

    zz = _dot(xn, wf_ref[...]) + bias_ref[...]
    log_f = jnp.minimum(zz, 0.0) - jnp.log1p(jnp.exp(-jnp.abs(zz)))
    c = _tri_cumsum(tri_ref[...], log_f, 3) + carry_s[...]
    tm = c.shape[0]
    carry_s[...] = c[tm - 1:tm, :]
    c_end[0] = c[tm - 1:tm, :]

    cc = jnp.where(lane < FOX_HEADS, c * LOG2E, 0.0)
    hi, mid, low = _split3(cc)
    c3 = (hi.astype(F32) + pltpu.roll(mid.astype(F32), C3_MID, axis=1)
          + pltpu.roll(low.astype(F32), C3_LO, axis=1) + jnp.where(lane == C3_ONE, 1.0, 0.0)).astype(BF16)

    def head_slabs(acc, gain, place_ref, out):
        ss = _dot((acc * acc).astype(BF16), bd_ref[...])
        r = lax.rsqrt(ss * (1.0 / FOX_HEAD_DIM) + EPS)
        r_hi = r.astype(BF16)
        r_lo = (r - r_hi.astype(F32)).astype(BF16)
        xh = acc * _dot(jnp.concatenate([r_hi, r_lo], axis=1), ex_ref[...]) * gain
        aug = _dot(c3, place_ref[...])
        for p in range(FOX_PAIRS):
            pair = slice(p * LANES, (p + 1) * LANES)
            out[2 * p] = jnp.where(lo, xh[:, pair], aug[:, pair]).astype(BF16)
            out[2 * p + 1] = jnp.where(lo, aug[:, pair], xh[:, pair]).astype(BF16)

    d = D_MODEL
    head_slabs(_dot(xn, w_ref[:, 0:d]), qg_ref[...], pq_ref, q_out)
    head_slabs(_dot(xn, w_ref[:, d:2 * d]), kg_ref[...], pk_ref, k_out)
    v_out[...] = _dot(xn, w_ref[:, 2 * d:3 * d]).astype(BF16)
    g_out[...] = _dot(xn, w_ref[:, 3 * d:4 * d])


def _fox_proj(x, seq, gain, w, wf, bias, c0, bd, ex, qg, kg, pq, pk):
    m = x.shape[0]
    tm = _row_tile(seq, ROW_TILE)
    d = D_MODEL
    row = lambda i: (i, 0)
    slab = pl.BlockSpec((FOX_HEADS, tm, LANES), lambda i: (0, i, 0))
    return pl.pallas_call(
        functools.partial(_fox_proj_kernel, seq // tm),
        grid=(m // tm,),
        in_specs=[
            pl.BlockSpec((tm, d), row),
            _resident((1, d)),
            _resident((d, 4 * d)),
            _resident((d, LANES)),
            _resident((1, LANES)),
            _resident((1, LANES)),
            _resident((tm, tm)),
            _resident((d, LANES)),
            _resident((2 * LANES, d)),
            _resident((1, d)),
            _resident((1, d)),
            _resident((LANES, d)),
            _resident((LANES, d)),
        ],
        out_specs=[slab, slab, pl.BlockSpec((tm, d), row), pl.BlockSpec((tm, d), row),
                   pl.BlockSpec((1, 1, LANES), lambda i: (i, 0, 0))],
        out_shape=[
            jax.ShapeDtypeStruct((FOX_HEADS, m, LANES), BF16),
            jax.ShapeDtypeStruct((FOX_HEADS, m, LANES), BF16),
            jax.ShapeDtypeStruct((m, d), BF16),
            jax.ShapeDtypeStruct((m, d), F32),
            jax.ShapeDtypeStruct((m // tm, 1, LANES), F32),
        ],
        scratch_shapes=[pltpu.VMEM((1, LANES), F32)],
        compiler_params=_params("arbitrary"),
        name="fox_proj",
    )(x, gain, w, wf, bias, c0, _block_tril(tm, tm), bd, ex, qg, kg, pq, pk)


def _bound_lane(h):
    return (FOX_HEAD_DIM if h % 2 == 0 else 0) + AUG_BOUND


def _placement(bound2):
    pq = np.zeros((LANES, D_MODEL), np.float32)
    pk = np.zeros((LANES, D_MODEL), np.float32)
    pb = np.zeros((LANES, D_MODEL), np.float32)
    for h in range(FOX_HEADS):
        base = (h // 2) * LANES + (FOX_HEAD_DIM if h % 2 == 0 else 0)
        for piece, src in enumerate((h, C3_MID + h, C3_LO + h)):
            pq[src, base + AUG_C + piece] = 1.0
            pk[src, base + AUG_ONE + piece] = -1.0
            pq[C3_ONE, base + AUG_ONE + piece] = 1.0
            pk[C3_ONE, base + AUG_C + piece] = 1.0
        pq[C3_ONE, base + AUG_BOUND] = 1.0
        pb[C3_ONE, base + AUG_BOUND] = 1.0
    return jnp.asarray(pq, BF16), (jnp.asarray(pk) - jnp.asarray(pb) * bound2).astype(BF16)


def _first_key_tile(c_tile_end, c_start, bsz):
    nt = c_tile_end.shape[0] // bsz
    ce = c_tile_end.reshape(bsz, nt, LANES)[:, :, :FOX_HEADS] * LOG2E
    before = jnp.concatenate([jnp.broadcast_to(c_start[:, :FOX_HEADS] * LOG2E, (bsz, 1, FOX_HEADS)),
                              ce[:, :-1]], axis=1)
    gap = before[:, :, None, :] - ce[:, None, :, :]
    earlier = (jnp.arange(nt)[None, :] < jnp.arange(nt)[:, None])[None, :, :, None]
    dead = jnp.logical_and(gap < -SKIP_BITS, earlier)
    dead = jnp.logical_and(dead[..., 0::2], dead[..., 1::2])
    first = jnp.sum(dead.astype(jnp.int32), axis=2)
    return first.transpose(0, 2, 1).reshape(-1)


def _fox_attn_kernel(has_meta, t, nt, lp, subs, flag_ref, first_ref, q_ref, k_ref, v_ref, gate_ref, *rest):
    if has_meta:
        km_ref, vm_ref, o_ref, m_s, l_s, lp_s, acc_s = rest
    else:
        o_ref, m_s, l_s, lp_s, acc_s = rest

    lane = lax.broadcasted_iota(jnp.int32, (1, LANES), 1)
    lo = lane < FOX_HEAD_DIM
    half = (lo, jnp.logical_not(lo))

    def run(online, sub):
        i = pl.program_id(2) * subs + sub
        rows = pl.ds(pl.multiple_of(sub * t, t), t)
        first = first_ref[(pl.program_id(0) * FOX_PAIRS + pl.program_id(1)) * nt + i]

        def tile(keys, vt, mask):
            alphas, pv = [], None
            zero_v = jnp.zeros_like(vt)
            for a in range(2):
                s = _dot(q_ref[a, 0, rows, :], keys(a), NT)
                if mask is not None:
                    s = jnp.where(mask, s, NEG_BIG)
                if online:
                    m_prev = m_s[a]
                    m_new = jnp.maximum(m_prev, jnp.max(s, axis=1, keepdims=True))
                    alpha = jnp.exp2(m_prev - m_new)
                    p = jnp.exp2(s - m_new)
                    l_s[a] = alpha * l_s[a] + jnp.sum(p, axis=1, keepdims=True)
                    m_s[a] = m_new
                    alphas.append(alpha)
                else:
                    p = jnp.exp2(s)
                    part = lp_s[a]
                    for c0 in range(0, p.shape[1], lp):
                        part = part + p[:, c0:c0 + lp]
                    lp_s[a] = part
                d = _dot(p.astype(BF16), jnp.where(half[a], vt, zero_v))
                pv = d if pv is None else pv + d
            if online:
                acc_s[...] = acc_s[...] * jnp.where(lo, alphas[0], alphas[1]) + pv
            else:
                acc_s[...] += pv

        def real_tile(off, size, mask):
            tile(lambda a: k_ref[a, 0, pl.ds(off, size), :], v_ref[0, pl.ds(off, size), :], mask)

        if online:
            m_s[...] = jnp.full(m_s.shape, NEG_BIG, F32)
            l_s[...] = jnp.zeros(l_s.shape, F32)
        else:
            lp_s[...] = jnp.zeros(lp_s.shape, F32)
        acc_s[...] = jnp.zeros(acc_s.shape, F32)

        if nt >= 2:
            def body(j, carry):
                real_tile(pl.multiple_of(j * (2 * t), 2 * t), 2 * t, None)
                return carry

            start = jnp.int32(0) if online else lax.shift_right_logical(first, 1)
            lax.fori_loop(start, lax.shift_right_logical(i, 1), body, 0)

            odd = (i & 1) == 1

            @pl.when(odd if online else jnp.logical_and(odd, first < i))
            def _():
                real_tile(pl.multiple_of((i - 1) * t, t), t, None)

        off = pl.multiple_of(i * t, t)

        def diag(n_pre):
            row = lax.broadcasted_iota(jnp.int32, (t, n_pre + t), 0)
            col = lax.broadcasted_iota(jnp.int32, (t, n_pre + t), 1)
            if n_pre:
                tile(lambda a: jnp.concatenate([km_ref[a], k_ref[a, 0, pl.ds(off, t), :]], axis=0),
                     jnp.concatenate([vm_ref[...], v_ref[0, pl.ds(off, t), :]], axis=0),
                     col - n_pre <= row)
            else:
                real_tile(off, t, col <= row)

        if has_meta and online:
            diag(LANES)
        elif has_meta:
            pl.when(first == 0)(lambda: diag(LANES))
            pl.when(first > 0)(lambda: diag(0))
        else:
            diag(0)

        if online:
            inv = jnp.where(lo, 1.0 / l_s[0], 1.0 / l_s[1])
        else:
            inv = jnp.where(lo, 1.0 / jnp.sum(lp_s[0], axis=1, keepdims=True),
                            1.0 / jnp.sum(lp_s[1], axis=1, keepdims=True))
        o_ref[0, rows, :] = (acc_s[...] * inv * _sigmoid(gate_ref[0, rows, :])).astype(BF16)

    def steps(online):
        if subs == 1:
            run(online, 0)
        else:
            def body(sub, carry):
                run(online, sub)
                return carry

            lax.fori_loop(0, subs, body, 0)

    @pl.when(flag_ref[0] == 1)
    def _():
        steps(False)

    @pl.when(flag_ref[0] != 1)
    def _():
        steps(True)


def _fox_attn(flag, first, q, k, v, gate, meta=None):
    b, l, d = v.shape
    t = _row_tile(l, ATTN_TILE)
    nt = l // t
    subs = ATTN_SUBTILES if nt % ATTN_SUBTILES == 0 else 1
    lp = LANES if t % LANES == 0 else t
    in_specs = [
        pl.BlockSpec(memory_space=pltpu.SMEM),
        pl.BlockSpec(memory_space=pltpu.SMEM),
        pl.BlockSpec((2, 1, subs * t, LANES), lambda bi, hp, i: (hp, bi, i, 0)),
        pl.BlockSpec((2, 1, l, LANES), lambda bi, hp, i: (hp, bi, 0, 0)),
        pl.BlockSpec((1, l, LANES), lambda bi, hp, i: (bi, 0, hp)),
        pl.BlockSpec((1, subs * t, LANES), lambda bi, hp, i: (bi, i, hp)),
    ]
    args = [flag, first, q, k, v, gate]
    if meta is not None:
        km, vm = meta
        in_specs += [
            pl.BlockSpec((2, LANES, LANES), lambda bi, hp, i: (hp, 0, 0)),
            pl.BlockSpec((LANES, LANES), lambda bi, hp, i: (0, hp)),
        ]
        args += [km, vm]
    return pl.pallas_call(
        functools.partial(_fox_attn_kernel, meta is not None, t, nt, lp, subs),
        grid=(b, FOX_PAIRS, nt // subs),
        in_specs=in_specs,
        out_specs=pl.BlockSpec((1, subs * t, LANES), lambda bi, hp, i: (bi, i, hp)),
        out_shape=jax.ShapeDtypeStruct((b, l, d), BF16),
        scratch_shapes=[
            pltpu.VMEM((2, t, 1), F32),
            pltpu.VMEM((2, t, 1), F32),
            pltpu.VMEM((2, t, lp), F32),
            pltpu.VMEM((t, LANES), F32),
        ],
        compiler_params=_params("arbitrary", "arbitrary", "arbitrary"),
        name="fox_attn",
    )(*args)


def _ffn_kernel(final, nh, y_ref, wmix_ref, h_ref, gain_ref, win_ref, wout_ref, fgain_ref, o_ref,
                hm_s, xn_s, acc_s):
    hm = h_ref[...] + _dot(y_ref[...], wmix_ref[...])
    hm_s[...] = hm
    xn_s[...] = _rms_rows(hm, gain_ref[...]).astype(BF16)
    acc_s[...] = jnp.zeros(acc_s.shape, F32)

    for j in range(nh):
        xn = xn_s[...]
        g = _dot(xn, win_ref[:, j * FFN_TILE:(j + 1) * FFN_TILE])
        u = _dot(xn, win_ref[:, FFN_HIDDEN + j * FFN_TILE:FFN_HIDDEN + (j + 1) * FFN_TILE])
        acc_s[...] += _dot((g * _sigmoid(g) * u).astype(BF16), wout_ref[j * FFN_TILE:(j + 1) * FFN_TILE, :])

    out = hm_s[...] + acc_s[...]
    if final:
        out = _rms_rows(out, fgain_ref[...])
    o_ref[...] = out


def _ffn(y, w_mix, h, gain, w_in, w_out, fgain, final):
    m, d = h.shape
    tm = _row_tile(m, ROW_TILE)
    nh = FFN_HIDDEN // FFN_TILE
    row = lambda i: (i, 0)
    return pl.pallas_call(
        functools.partial(_ffn_kernel, final, nh),
        grid=(m // tm,),
        in_specs=[
            pl.BlockSpec((tm, d), row),
            _resident((d, d)),
            pl.BlockSpec((tm, d), row),
            _resident((1, d)),
            _resident((d, 2 * FFN_HIDDEN)),
            _resident((FFN_HIDDEN, d)),
            _resident((1, d)),
        ],
        out_specs=pl.BlockSpec((tm, d), row),
        out_shape=jax.ShapeDtypeStruct((m, d), F32),
        scratch_shapes=[pltpu.VMEM((tm, d), F32), pltpu.VMEM((tm, d), BF16), pltpu.VMEM((tm, d), F32)],
        compiler_params=_params("arbitrary"),
        name="ffn",
    )(y, w_mix, h, gain, w_in, w_out, fgain)


def _hgrn_proj_kernel(x_ref, gain_ref, w_ref, q_out, z_out, i_out, g_out):
    d = D_MODEL
    xn = _rms_rows(x_ref[...], gain_ref[...]).astype(BF16)
    q_out[...] = _dot(xn, w_ref[:, 0:d]).astype(BF16)
    z_out[...] = _dot(xn, w_ref[:, d:2 * d])
    i_out[...] = _dot(xn, w_ref[:, 2 * d:3 * d]).astype(BF16)
    g_out[...] = _dot(xn, w_ref[:, 3 * d:4 * d]).astype(BF16)


def _hgrn_proj(x, gain, w):
    m, d = x.shape
    tm = _row_tile(m, ROW_TILE)
    row = lambda i: (i, 0)
    return pl.pallas_call(
        _hgrn_proj_kernel,
        grid=(m // tm,),
        in_specs=[pl.BlockSpec((tm, d), row), _resident((1, d)), _resident((d, 4 * d))],
        out_specs=[pl.BlockSpec((tm, d), row)] * 4,
        out_shape=[jax.ShapeDtypeStruct((m, d), BF16), jax.ShapeDtypeStruct((m, d), F32),
                   jax.ShapeDtypeStruct((m, d), BF16), jax.ShapeDtypeStruct((m, d), BF16)],
        compiler_params=_params("arbitrary"),
        name="hgrn_proj",
    )(x, gain, w)


def _hgrn_scan_kernel(t, c, q_ref, z_ref, i_ref, g_ref, lbp_ref, gg_ref, tri_ref, s0_ref,
                      y_ref, sfin_ref, st_s, oi_s, oa_s, qin_s, kout_s, qf_s, kf_s, b_s, q_s, k_s, v_s):
    step = pl.program_id(1)
    nc = t // c
    mid = c // 2
    heads = [slice(h * HGRN_DK, (h + 1) * HGRN_DK) for h in range(HGRN_HEADS)]

    @pl.when(step == 0)
    def _():
        st_s[...] = s0_ref[...]

    lbp = lbp_ref[...]
    mx = jnp.maximum(lbp[0:1], lbp[1:2])
    e0 = jnp.exp(lbp[0:1] - mx)
    e1 = jnp.exp(lbp[1:2] - mx)
    den = e0 + e1
    sm0 = e0 / den
    lb = (sm0 + e1 / den) - sm0

    sig = _sigmoid(z_ref[0])
    log_f = jnp.log2(lb + (1.0 - lb) * sig)
    kk = (1.0 - lb) * (1.0 - sig)
    qz = q_ref[0].astype(F32)
    qq = qz * _sigmoid(qz)
    v_b = i_ref[0]

    b = _tri_cumsum(tri_ref[...], log_f, 2)

    spread = jnp.float32(0.0)
    decay = []
    for ci in range(nc):
        rs = slice(ci * c, (ci + 1) * c)
        bc = b[rs]
        b_last = bc[c - 1:c]
        b_mid = bc[mid - 1:mid]
        spread = jnp.maximum(spread, jnp.maximum(jnp.max(bc[0:1] - b_mid), jnp.max(b_mid - b_last)))
        q_in = qq[rs] * jnp.exp2(bc)
        k_out = kk[rs] * jnp.exp2(b_last - bc)
        qin_s[rs] = q_in.astype(BF16)
        kout_s[rs] = k_out.astype(BF16)
        qf_s[rs] = (q_in * jnp.exp2(-b_mid)).astype(BF16)
        kf_s[rs] = (k_out * jnp.exp2(b_mid - b_last)).astype(BF16)
        decay.append(jnp.exp2(b_last))

    keep = tri_ref[...] > 0
    for sl in heads:
        a = jnp.where(keep, _dot(qf_s[:, sl], kf_s[:, sl], NT), 0.0)
        oa_s[:, sl] = _dot(a.astype(BF16), v_b[:, sl])

    for ci in range(nc):
        rs = slice(ci * c, (ci + 1) * c)
        for h, sl in enumerate(heads):
            st = st_s[h]
            oi_s[rs, sl] = _dot(qin_s[rs, sl], st.astype(BF16), NT)
            st_s[h] = st * decay[ci][:, sl] + _dot(v_b[rs, sl], kout_s[rs, sl], TN)

    @pl.when(spread > FAST_RANGE)
    def _():
        b_s[...] = b
        k_s[...] = kk
        q_s[...] = qq
        v_s[...] = v_b.astype(F32)
        oa_s[...] = jnp.zeros(oa_s.shape, F32)
        rowc = lax.broadcasted_iota(jnp.int32, (c, 1), 0)

        def body(s, carry):
            base = pl.multiple_of((s // c) * c, c)
            blk = pl.ds(base, c)
            p = (q_s[blk, :] * jnp.exp2(jnp.minimum(b_s[blk, :] - b_s[pl.ds(s, 1), :], 0.0))
                 * k_s[pl.ds(s, 1), :])
            vs = v_s[pl.ds(s, 1), :]
            seen = rowc + base >= s
            for sl in heads:
                w = jnp.where(seen, jnp.sum(p[:, sl], axis=1, keepdims=True), 0.0)
                oa_s[blk, sl] += w * vs[:, sl]
            return carry

        lax.fori_loop(0, t, body, 0)

    gz = g_ref[0].astype(F32)
    gate = gz * _sigmoid(gz)
    o = oi_s[...] + oa_s[...]
    for sl in heads:
        y_ref[0, :, sl] = (_rms_rows(o[:, sl], gg_ref[:, sl]) * gate[:, sl]).astype(BF16)

    @pl.when(step == pl.num_programs(1) - 1)
    def _():
        sfin_ref[...] = st_s[...]


def _hgrn_scan(q, z, i, g, lbp, gg, s0):
    b, l, d = q.shape
    t = _row_tile(l, SCAN_TILE)
    c = _row_tile(t, HGRN_CHUNK)
    blk = lambda bi, s: (bi, s, 0)
    st_shape = (HGRN_HEADS, HGRN_DK, HGRN_DK)
    return pl.pallas_call(
        functools.partial(_hgrn_scan_kernel, t, c),
        grid=(b, l // t),
        in_specs=[
            pl.BlockSpec((1, t, d), blk),
            pl.BlockSpec((1, t, d), blk),
            pl.BlockSpec((1, t, d), blk),
            pl.BlockSpec((1, t, d), blk),
            _resident((2, d)),
            _resident((1, d)),
            _resident((t, t)),
            _resident(st_shape),
        ],
        out_specs=[pl.BlockSpec((1, t, d), blk), pl.BlockSpec(st_shape, lambda bi, s: (0, 0, 0))],
        out_shape=[jax.ShapeDtypeStruct((b, l, d), BF16), jax.ShapeDtypeStruct(st_shape, F32)],
        scratch_shapes=([pltpu.VMEM(st_shape, F32)] + [pltpu.VMEM((t, d), F32)] * 2
                        + [pltpu.VMEM((t, d), BF16)] * 4 + [pltpu.VMEM((t, d), F32)] * 4),
        compiler_params=_params("arbitrary", "arbitrary"),
        name="hgrn_scan",
    )(q, z, i, g, lbp, gg, _block_tril(t, c), s0)


def kernel(x, meta_tokens, attn_norm, ffn_norm, final_norm, fox_w_in, fox_b_f, fox_q_norm, fox_k_norm,
           fox_w_out, hgrn_w_in, hgrn_lower_bounds, hgrn_g_norm, hgrn_w_out, ffn_w_in, ffn_w_out):
    bsz, seq, d = x.shape
    row = lambda v: v.reshape(1, -1).astype(F32)

    order = jnp.argsort(fox_b_f[0])
    cols = (order[:, None] * FOX_HEAD_DIM + jnp.arange(FOX_HEAD_DIM)[None, :]).reshape(-1)
    fox_b_sorted = fox_b_f[0][order]

    w_fox = fox_w_in[0, :, :4 * d].reshape(d, 4, d)[:, :, cols].reshape(d, 4 * d).astype(BF16)
    w_f = jnp.pad(fox_w_in[0, :, 4 * d:][:, order], ((0, 0), (0, LANES - FOX_HEADS))).astype(BF16)
    w_fox_out = fox_w_out[0][cols, :].astype(BF16)
    w_hgrn = hgrn_w_in[0].astype(BF16)
    w_hgrn_out = hgrn_w_out[0].astype(BF16)
    w_ffn_in = ffn_w_in.astype(BF16)
    w_ffn_out = ffn_w_out.astype(BF16)
    head_of = np.arange(d) // FOX_HEAD_DIM
    bd = jnp.asarray(head_of[:, None] == np.arange(LANES)[None, :], BF16)
    ex = jnp.concatenate([bd.T, bd.T], axis=0)
    qg = row(jnp.tile(fox_q_norm[0], FOX_HEADS)) * (FOX_HEAD_DIM ** -0.5 * LOG2E)
    kg = row(jnp.tile(fox_k_norm[0], FOX_HEADS))
    bound2 = (1.01 * LOG2E * FOX_HEAD_DIM ** 0.5 * jnp.max(jnp.abs(fox_q_norm[0]))
              * jnp.max(jnp.abs(fox_k_norm[0]))).astype(BF16).astype(F32)
    flag = (bound2 <= BOUND_LIMIT).astype(jnp.int32).reshape(1)
    pq, pk = _placement(bound2)
    b_f = jnp.pad(row(fox_b_sorted), ((0, 0), (0, LANES - FOX_HEADS)))
    gg = row(jnp.tile(hgrn_g_norm[0], HGRN_HEADS))
    lbp = hgrn_lower_bounds.astype(F32)

    h_meta = meta_tokens.astype(F32)
    h_real = x.reshape(bsz * seq, d)
    seq3 = lambda a: a.reshape(bsz, seq, -1)

    proj = lambda h, n, c0: _fox_proj(h, n, row(attn_norm[0]), w_fox, w_f, b_f, c0, bd, ex, qg, kg, pq, pk)
    qm, km, vm, gm, c_meta = proj(h_meta, N_META, jnp.zeros((1, LANES), F32))
    qr, kr, vr, gr, c_tiles = proj(h_real, seq, c_meta[0])
    ym = _fox_attn(flag, jnp.zeros((FOX_PAIRS,), jnp.int32), qm[:, None], km[:, None], vm[None], gm[None])[0]
    pad_bias = np.zeros((FOX_HEADS, LANES, LANES), np.float32)
    for h in range(FOX_HEADS):
        pad_bias[h, N_META:, _bound_lane(h)] = PAD_KEY_BIAS
    km_pad = jnp.pad(km, ((0, 0), (0, LANES - N_META), (0, 0))) + jnp.asarray(pad_bias, BF16)
    vm_pad = jnp.pad(vm, ((0, LANES - N_META), (0, 0)))
    heads4 = lambda a: a.reshape(FOX_HEADS, bsz, seq, LANES)
    yr = _fox_attn(flag, _first_key_tile(c_tiles, c_meta[0], bsz), heads4(qr), heads4(kr), seq3(vr), seq3(gr),
                   meta=(km_pad, vm_pad)).reshape(bsz * seq, d)
    ffn0 = lambda y, h: _ffn(y, w_fox_out, h, row(ffn_norm[0]), w_ffn_in[0], w_ffn_out[0],
                             row(final_norm), False)
    h_meta = ffn0(ym, h_meta)
    h_real = ffn0(yr, h_real)

    proj = lambda h: _hgrn_proj(h, row(attn_norm[1]), w_hgrn)
    qm, zm, im, gm = proj(h_meta)
    qr, zr, ir, gr = proj(h_real)
    s0 = jnp.zeros((HGRN_HEADS, HGRN_DK, HGRN_DK), F32)
    _, s_meta = _hgrn_scan(qm[None], zm[None], im[None], gm[None], lbp, gg, s0)
    yr, _ = _hgrn_scan(seq3(qr), seq3(zr), seq3(ir), seq3(gr), lbp, gg, s_meta)
    out = _ffn(yr.reshape(bsz * seq, d), w_hgrn_out, h_real, row(ffn_norm[1]), w_ffn_in[1], w_ffn_out[1],
               row(final_norm), True)
    return out.reshape(bsz, seq, d)
```

```python
import functools
import math

import numpy as np
import jax
import jax.numpy as jnp
from jax import lax
from jax.experimental import pallas as pl
from jax.experimental.pallas import tpu as pltpu

D_MODEL = 1024
N_META = 16
FOX_HEADS = 16
FOX_HEAD_DIM = 64
HGRN_HEADS = 8
HGRN_DK = 128
HGRN_CHUNK = 64
FFN_HIDDEN = 2816
EPS = 1e-6

LANES = 128
FOX_PAIRS = FOX_HEADS // 2
FFN_TILE = 256
NEG_BIG = -1e30
PAD_KEY_BIAS = -30000.0
LOG2E = math.log2(math.e)
BOUND_LIMIT = 56.0
SKIP_BITS = 152.0
FAST_RANGE = 56.0
VMEM_LIMIT = 52 * 1024 * 1024

ROW_TILE = 512
ATTN_TILE = 512
ATTN_SUBTILES = 2
SCAN_TILE = 256

AUG_C = 0
AUG_ONE = 3
AUG_BOUND = 6
C3_MID, C3_LO, C3_ONE = 16, 32, 48

F32 = jnp.float32
BF16 = jnp.bfloat16

NN = (((1,), (0,)), ((), ()))
NT = (((1,), (1,)), ((), ()))
TN = (((0,), (0,)), ((), ()))


def _dot(a, b, dims=NN):
    return lax.dot_general(a, b, dims, preferred_element_type=F32)


def _params(*sem):
    return pltpu.CompilerParams(dimension_semantics=sem, vmem_limit_bytes=VMEM_LIMIT)


def _row_tile(m, pref):
    return pref if m % pref == 0 else m


def _resident(shape):
    return pl.BlockSpec(shape, lambda *_: (0,) * len(shape), pipeline_mode=pl.Buffered(1))


def _sigmoid(x):
    return 1.0 / (1.0 + jnp.exp(-x))


def _rms_rows(x, gain):
    ms = jnp.mean(x * x, axis=-1, keepdims=True)
    return x * lax.rsqrt(ms + EPS) * gain


def _split3(x):
    hi = x.astype(BF16)
    r1 = x - hi.astype(F32)
    mid = r1.astype(BF16)
    return hi, mid, (r1 - mid.astype(F32)).astype(BF16)


def _tri_cumsum(tri, x, pieces):
    if pieces == 3:
        hi, mid, lo = _split3(x)
        return _dot(tri, hi) + _dot(tri, mid) + _dot(tri, lo)
    hi = x.astype(BF16)
    return _dot(tri, hi) + _dot(tri, (x - hi.astype(F32)).astype(BF16))


def _block_tril(t, c):
    r = np.arange(t)
    return jnp.asarray((r[None, :] <= r[:, None]) & (r[None, :] // c == r[:, None] // c), BF16)


def _fox_proj_kernel(tiles_per_seq, x_ref, gain_ref, w_ref, wf_ref, bias_ref, c0_ref, tri_ref, bd_ref, ex_ref,
                     qg_ref, kg_ref, pq_ref, pk_ref, q_out, k_out, v_out, g_out, c_end, carry_s):
    @pl.when(lax.rem(pl.program_id(0), tiles_per_seq) == 0)
    def _():
        carry_s[...] = c0_ref[...]

    xn = _rms_rows(x_ref[...], gain_ref[...]).astype(BF16)
    lane = lax.broadcasted_iota(jnp.int32, (1, LANES), 1)
    lo = lane < FOX_HEAD_DIM

    zz = _dot(xn, wf_ref[...]) + bias_ref[...]
    log_f = jnp.minimum(zz, 0.0) - jnp.log1p(jnp.exp(-jnp.abs(zz)))
    c = _tri_cumsum(tri_ref[...], log_f, 3) + carry_s[...]
    tm = c.shape[0]
    carry_s[...] = c[tm - 1:tm, :]
    c_end[0] = c[tm - 1:tm, :]

    cc = jnp.where(lane < FOX_HEADS, c * LOG2E, 0.0)
    hi, mid, low = _split3(cc)
    c3 = (hi.astype(F32) + pltpu.roll(mid.astype(F32), C3_MID, axis=1)
          + pltpu.roll(low.astype(F32), C3_LO, axis=1) + jnp.where(lane == C3_ONE, 1.0, 0.0)).astype(BF16)

    def head_slabs(acc, gain, place_ref, out):
        ss = _dot((acc * acc).astype(BF16), bd_ref[...])
        r = lax.rsqrt(ss * (1.0 / FOX_HEAD_DIM) + EPS)
        r_hi = r.astype(BF16)
        r_lo = (r - r_hi.astype(F32)).astype(BF16)
        xh = acc * _dot(jnp.concatenate([r_hi, r_lo], axis=1), ex_ref[...]) * gain
        aug = _dot(c3, place_ref[...])
        for p in range(FOX_PAIRS):
            pair = slice(p * LANES, (p + 1) * LANES)
            out[2 * p] = jnp.where(lo, xh[:, pair], aug[:, pair]).astype(BF16)
            out[2 * p + 1] = jnp.where(lo, aug[:, pair], xh[:, pair]).astype(BF16)

    d = D_MODEL
    head_slabs(_dot(xn, w_ref[:, 0:d]), qg_ref[...], pq_ref, q_out)
    head_slabs(_dot(xn, w_ref[:, d:2 * d]), kg_ref[...], pk_ref, k_out)
    v_out[...] = _dot(xn, w_ref[:, 2 * d:3 * d]).astype(BF16)
    g_out[...] = _dot(xn, w_ref[:, 3 * d:4 * d])


def _fox_proj(x, seq, gain, w, wf, bias, c0, bd, ex, qg, kg, pq, pk):
    m = x.shape[0]
    tm = _row_tile(seq, ROW_TILE)
    d = D_MODEL
    row = lambda i: (i, 0)
    slab = pl.BlockSpec((FOX_HEADS, tm, LANES), lambda i: (0, i, 0))
    return pl.pallas_call(
        functools.partial(_fox_proj_kernel, seq // tm),
        grid=(m // tm,),
        in_specs=[
            pl.BlockSpec((tm, d), row),
            _resident((1, d)),
            _resident((d, 4 * d)),
            _resident((d, LANES)),
            _resident((1, LANES)),
            _resident((1, LANES)),
            _resident((tm, tm)),
            _resident((d, LANES)),
            _resident((2 * LANES, d)),
            _resident((1, d)),
            _resident((1, d)),
            _resident((LANES, d)),
            _resident((LANES, d)),
        ],
        out_specs=[slab, slab, pl.BlockSpec((tm, d), row), pl.BlockSpec((tm, d), row),
                   pl.BlockSpec((1, 1, LANES), lambda i: (i, 0, 0))],
        out_shape=[
            jax.ShapeDtypeStruct((FOX_HEADS, m, LANES), BF16),
            jax.ShapeDtypeStruct((FOX_HEADS, m, LANES), BF16),
            jax.ShapeDtypeStruct((m, d), BF16),
            jax.ShapeDtypeStruct((m, d), F32),
            jax.ShapeDtypeStruct((m // tm, 1, LANES), F32),
        ],
        scratch_shapes=[pltpu.VMEM((1, LANES), F32)],
        compiler_params=_params("arbitrary"),
        name="fox_proj",
    )(x, gain, w, wf, bias, c0, _block_tril(tm, tm), bd, ex, qg, kg, pq, pk)


def _bound_lane(h):
    return (FOX_HEAD_DIM if h % 2 == 0 else 0) + AUG_BOUND


def _placement(bound2):
    pq = np.zeros((LANES, D_MODEL), np.float32)
    pk = np.zeros((LANES, D_MODEL), np.float32)
    pb = np.zeros((LANES, D_MODEL), np.float32)
    for h in range(FOX_HEADS):
        base = (h // 2) * LANES + (FOX_HEAD_DIM if h % 2 == 0 else 0)
        for piece, src in enumerate((h, C3_MID + h, C3_LO + h)):
            pq[src, base + AUG_C + piece] = 1.0
            pk[src, base + AUG_ONE + piece] = -1.0
            pq[C3_ONE, base + AUG_ONE + piece] = 1.0
            pk[C3_ONE, base + AUG_C + piece] = 1.0
        pq[C3_ONE, base + AUG_BOUND] = 1.0
        pb[C3_ONE, base + AUG_BOUND] = 1.0
    return jnp.asarray(pq, BF16), (jnp.asarray(pk) - jnp.asarray(pb) * bound2).astype(BF16)


def _first_key_tile(c_tile_end, c_start, bsz):
    nt = c_tile_end.shape[0] // bsz
    ce = c_tile_end.reshape(bsz, nt, LANES)[:, :, :FOX_HEADS] * LOG2E
    before = jnp.concatenate([jnp.broadcast_to(c_start[:, :FOX_HEADS] * LOG2E, (bsz, 1, FOX_HEADS)),
                              ce[:, :-1]], axis=1)
    gap = before[:, :, None, :] - ce[:, None, :, :]
    earlier = (jnp.arange(nt)[None, :] < jnp.arange(nt)[:, None])[None, :, :, None]
    dead = jnp.logical_and(gap < -SKIP_BITS, earlier)
    dead = jnp.logical_and(dead[..., 0::2], dead[..., 1::2])
    first = jnp.sum(dead.astype(jnp.int32), axis=2)
    return first.transpose(0, 2, 1).reshape(-1)


def _fox_attn_kernel(has_meta, t, nt, lp, subs, flag_ref, first_ref, q_ref, k_ref, v_ref, gate_ref, *rest):
    if has_meta:
        km_ref, vm_ref, o_ref, m_s, l_s, lp_s, acc_s = rest
    else:
        o_ref, m_s, l_s, lp_s, acc_s = rest

    lane = lax.broadcasted_iota(jnp.int32, (1, LANES), 1)
    lo = lane < FOX_HEAD_DIM
    half = (lo, jnp.logical_not(lo))

    def run(online, sub):
        i = pl.program_id(2) * subs + sub
        rows = pl.ds(pl.multiple_of(sub * t, t), t)
        first = first_ref[(pl.program_id(0) * FOX_PAIRS + pl.program_id(1)) * nt + i]

        def tile(keys, vt, mask):
            alphas, pv = [], None
            zero_v = jnp.zeros_like(vt)
            for a in range(2):
                s = _dot(q_ref[a, 0, rows, :], keys(a), NT)
                if mask is not None:
                    s = jnp.where(mask, s, NEG_BIG)
                if online:
                    m_prev = m_s[a]
                    m_new = jnp.maximum(m_prev, jnp.max(s, axis=1, keepdims=True))
                    alpha = jnp.exp2(m_prev - m_new)
                    p = jnp.exp2(s - m_new)
                    l_s[a] = alpha * l_s[a] + jnp.sum(p, axis=1, keepdims=True)
                    m_s[a] = m_new
                    alphas.append(alpha)
                else:
                    p = jnp.exp2(s)
                    part = lp_s[a]
                    for c0 in range(0, p.shape[1], lp):
                        part = part + p[:, c0:c0 + lp]
                    lp_s[a] = part
                d = _dot(p.astype(BF16), jnp.where(half[a], vt, zero_v))
                pv = d if pv is None else pv + d
            if online:
                acc_s[...] = acc_s[...] * jnp.where(lo, alphas[0], alphas[1]) + pv
            else:
                acc_s[...] += pv

        def real_tile(off, size, mask):
            tile(lambda a: k_ref[a, 0, pl.ds(off, size), :], v_ref[0, pl.ds(off, size), :], mask)

        if online:
            m_s[...] = jnp.full(m_s.shape, NEG_BIG, F32)
            l_s[...] = jnp.zeros(l_s.shape, F32)
        else:
            lp_s[...] = jnp.zeros(lp_s.shape, F32)
        acc_s[...] = jnp.zeros(acc_s.shape, F32)

        if nt >= 2:
            lead = jnp.int32(0) if online else first
            n = i - lead

            @pl.when((n & 1) == 1)
            def _():
                real_tile(pl.multiple_of(lead * t, t), t, None)

            def body(j, carry):
                real_tile(pl.multiple_of((lead + (n & 1) + 2 * j) * t, t), 2 * t, None)
                return carry

            lax.fori_loop(0, lax.shift_right_logical(n, 1), body, 0)

        off = pl.multiple_of(i * t, t)

        def diag(n_pre):
            row = lax.broadcasted_iota(jnp.int32, (t, n_pre + t), 0)
            col = lax.broadcasted_iota(jnp.int32, (t, n_pre + t), 1)
            if n_pre:
                tile(lambda a: jnp.concatenate([km_ref[a], k_ref[a, 0, pl.ds(off, t), :]], axis=0),
                     jnp.concatenate([vm_ref[...], v_ref[0, pl.ds(off, t), :]], axis=0),
                     col - n_pre <= row)
            else:
                real_tile(off, t, col <= row)

        if has_meta and online:
            diag(LANES)
        elif has_meta:
            pl.when(first == 0)(lambda: diag(LANES))
            pl.when(first > 0)(lambda: diag(0))
        else:
            diag(0)

        if online:
            inv = jnp.where(lo, 1.0 / l_s[0], 1.0 / l_s[1])
        else:
            inv = jnp.where(lo, 1.0 / jnp.sum(lp_s[0], axis=1, keepdims=True),
                            1.0 / jnp.sum(lp_s[1], axis=1, keepdims=True))
        o_ref[0, rows, :] = (acc_s[...] * inv * _sigmoid(gate_ref[0, rows, :])).astype(BF16)

    def steps(online):
        if subs == 1:
            run(online, 0)
        else:
            def body(sub, carry):
                run(online, sub)
                return carry

            lax.fori_loop(0, subs, body, 0)

    @pl.when(flag_ref[0] == 1)
    def _():
        steps(False)

    @pl.when(flag_ref[0] != 1)
    def _():
        steps(True)


def _fox_attn(flag, first, q, k, v, gate, meta=None):
    b, l, d = v.shape
    t = _row_tile(l, ATTN_TILE)
    nt = l // t
    subs = ATTN_SUBTILES if nt % ATTN_SUBTILES == 0 else 1
    lp = LANES if t % LANES == 0 else t
    in_specs = [
        pl.BlockSpec(memory_space=pltpu.SMEM),
        pl.BlockSpec(memory_space=pltpu.SMEM),
        pl.BlockSpec((2, 1, subs * t, LANES), lambda bi, hp, i: (hp, bi, i, 0)),
        pl.BlockSpec((2, 1, l, LANES), lambda bi, hp, i: (hp, bi, 0, 0)),
        pl.BlockSpec((1, l, LANES), lambda bi, hp, i: (bi, 0, hp)),
        pl.BlockSpec((1, subs * t, LANES), lambda bi, hp, i: (bi, i, hp)),
    ]
    args = [flag, first, q, k, v, gate]
    if meta is not None:
        km, vm = meta
        in_specs += [
            pl.BlockSpec((2, LANES, LANES), lambda bi, hp, i: (hp, 0, 0)),
            pl.BlockSpec((LANES, LANES), lambda bi, hp, i: (0, hp)),
        ]
        args += [km, vm]
    return pl.pallas_call(
        functools.partial(_fox_attn_kernel, meta is not None, t, nt, lp, subs),
        grid=(b, FOX_PAIRS, nt // subs),
        in_specs=in_specs,
        out_specs=pl.BlockSpec((1, subs * t, LANES), lambda bi, hp, i: (bi, i, hp)),
        out_shape=jax.ShapeDtypeStruct((b, l, d), BF16),
        scratch_shapes=[
            pltpu.VMEM((2, t, 1), F32),
            pltpu.VMEM((2, t, 1), F32),
            pltpu.VMEM((2, t, lp), F32),
            pltpu.VMEM((t, LANES), F32),
        ],
        compiler_params=_params("arbitrary", "arbitrary", "arbitrary"),
        name="fox_attn",
    )(*args)


def _ffn_kernel(final, nh, y_ref, wmix_ref, h_ref, gain_ref, win_ref, wout_ref, fgain_ref, o_ref,
                hm_s, xn_s, acc_s):
    hm = h_ref[...] + _dot(y_ref[...], wmix_ref[...])
    hm_s[...] = hm
    xn_s[...] = _rms_rows(hm, gain_ref[...]).astype(BF16)
    acc_s[...] = jnp.zeros(acc_s.shape, F32)

    for j in range(nh):
        xn = xn_s[...]
        g = _dot(xn, win_ref[:, j * FFN_TILE:(j + 1) * FFN_TILE])
        u = _dot(xn, win_ref[:, FFN_HIDDEN + j * FFN_TILE:FFN_HIDDEN + (j + 1) * FFN_TILE])
        acc_s[...] += _dot((g * _sigmoid(g) * u).astype(BF16), wout_ref[j * FFN_TILE:(j + 1) * FFN_TILE, :])

    out = hm_s[...] + acc_s[...]
    if final:
        out = _rms_rows(out, fgain_ref[...])
    o_ref[...] = out


def _ffn(y, w_mix, h, gain, w_in, w_out, fgain, final):
    m, d = h.shape
    tm = _row_tile(m, ROW_TILE)
    nh = FFN_HIDDEN // FFN_TILE
    row = lambda i: (i, 0)
    return pl.pallas_call(
        functools.partial(_ffn_kernel, final, nh),
        grid=(m // tm,),
        in_specs=[
            pl.BlockSpec((tm, d), row),
            _resident((d, d)),
            pl.BlockSpec((tm, d), row),
            _resident((1, d)),
            _resident((d, 2 * FFN_HIDDEN)),
            _resident((FFN_HIDDEN, d)),
            _resident((1, d)),
        ],
        out_specs=pl.BlockSpec((tm, d), row),
        out_shape=jax.ShapeDtypeStruct((m, d), F32),
        scratch_shapes=[pltpu.VMEM((tm, d), F32), pltpu.VMEM((tm, d), BF16), pltpu.VMEM((tm, d), F32)],
        compiler_params=_params("arbitrary"),
        name="ffn",
    )(y, w_mix, h, gain, w_in, w_out, fgain)


def _hgrn_proj_kernel(x_ref, gain_ref, w_ref, lbp_ref, q_out, lf_out, k_out, v_out, g_out):
    d = D_MODEL
    xn = _rms_rows(x_ref[...], gain_ref[...]).astype(BF16)

    lbp = lbp_ref[...]
    mx = jnp.maximum(lbp[0:1], lbp[1:2])
    e0 = jnp.exp(lbp[0:1] - mx)
    e1 = jnp.exp(lbp[1:2] - mx)
    den = e0 + e1
    sm0 = e0 / den
    lb = (sm0 + e1 / den) - sm0

    qz = _dot(xn, w_ref[:, 0:d])
    q_out[...] = (qz * _sigmoid(qz)).astype(BF16)
    sig = _sigmoid(_dot(xn, w_ref[:, d:2 * d]))
    lf_out[...] = jnp.log2(lb + (1.0 - lb) * sig)
    k_out[...] = (1.0 - lb) * (1.0 - sig)
    v_out[...] = _dot(xn, w_ref[:, 2 * d:3 * d]).astype(BF16)
    gz = _dot(xn, w_ref[:, 3 * d:4 * d])
    g_out[...] = (gz * _sigmoid(gz)).astype(BF16)


def _hgrn_proj(x, gain, w, lbp):
    m, d = x.shape
    tm = _row_tile(m, ROW_TILE)
    row = lambda i: (i, 0)
    return pl.pallas_call(
        _hgrn_proj_kernel,
        grid=(m // tm,),
        in_specs=[pl.BlockSpec((tm, d), row), _resident((1, d)), _resident((d, 4 * d)), _resident((2, d))],
        out_specs=[pl.BlockSpec((tm, d), row)] * 5,
        out_shape=[jax.ShapeDtypeStruct((m, d), BF16), jax.ShapeDtypeStruct((m, d), F32),
                   jax.ShapeDtypeStruct((m, d), F32), jax.ShapeDtypeStruct((m, d), BF16),
                   jax.ShapeDtypeStruct((m, d), BF16)],
        compiler_params=_params("arbitrary"),
        name="hgrn_proj",
    )(x, gain, w, lbp)


def _hgrn_scan_kernel(t, c, q_ref, lf_ref, k_ref, i_ref, g_ref, gg_ref, tri_ref, s0_ref,
                      y_ref, sfin_ref, st_s, oi_s, oa_s, qin_s, kout_s, qf_s, kf_s, b_s, q_s, k_s, v_s):
    step = pl.program_id(1)
    nc = t // c
    mid = c // 2
    heads = [slice(h * HGRN_DK, (h + 1) * HGRN_DK) for h in range(HGRN_HEADS)]

    @pl.when(step == 0)
    def _():
        st_s[...] = s0_ref[...]

    qq = q_ref[0].astype(F32)
    kk = k_ref[0]
    v_b = i_ref[0]

    b = _tri_cumsum(tri_ref[...], lf_ref[0], 2)

    spread = jnp.float32(0.0)
    decay = []
    for ci in range(nc):
        rs = slice(ci * c, (ci + 1) * c)
        bc = b[rs]
        b_last = bc[c - 1:c]
        b_mid = bc[mid - 1:mid]
        spread = jnp.maximum(spread, jnp.maximum(jnp.max(bc[0:1] - b_mid), jnp.max(b_mid - b_last)))
        q_in = qq[rs] * jnp.exp2(bc)
        k_out = kk[rs] * jnp.exp2(b_last - bc)
        qin_s[rs] = q_in.astype(BF16)
        kout_s[rs] = k_out.astype(BF16)
        qf_s[rs] = (q_in * jnp.exp2(-b_mid)).astype(BF16)
        kf_s[rs] = (k_out * jnp.exp2(b_mid - b_last)).astype(BF16)
        decay.append(jnp.exp2(b_last))

    keep = tri_ref[...] > 0
    for sl in heads:
        a = jnp.where(keep, _dot(qf_s[:, sl], kf_s[:, sl], NT), 0.0)
        oa_s[:, sl] = _dot(a.astype(BF16), v_b[:, sl])

    for ci in range(nc):
        rs = slice(ci * c, (ci + 1) * c)
        for h, sl in enumerate(heads):
            st = st_s[h]
            oi_s[rs, sl] = _dot(qin_s[rs, sl], st.astype(BF16), NT)
            st_s[h] = st * decay[ci][:, sl] + _dot(v_b[rs, sl], kout_s[rs, sl], TN)

    @pl.when(spread > FAST_RANGE)
    def _():
        b_s[...] = b
        k_s[...] = kk
        q_s[...] = qq
        v_s[...] = v_b.astype(F32)
        oa_s[...] = jnp.zeros(oa_s.shape, F32)
        rowc = lax.broadcasted_iota(jnp.int32, (c, 1), 0)

        def body(s, carry):
            base = pl.multiple_of((s // c) * c, c)
            blk = pl.ds(base, c)
            p = (q_s[blk, :] * jnp.exp2(jnp.minimum(b_s[blk, :] - b_s[pl.ds(s, 1), :], 0.0))
                 * k_s[pl.ds(s, 1), :])
            vs = v_s[pl.ds(s, 1), :]
            seen = rowc + base >= s
            for sl in heads:
                w = jnp.where(seen, jnp.sum(p[:, sl], axis=1, keepdims=True), 0.0)
                oa_s[blk, sl] += w * vs[:, sl]
            return carry

        lax.fori_loop(0, t, body, 0)

    gate = g_ref[0].astype(F32)
    o = oi_s[...] + oa_s[...]
    for sl in heads:
        y_ref[0, :, sl] = (_rms_rows(o[:, sl], gg_ref[:, sl]) * gate[:, sl]).astype(BF16)

    @pl.when(step == pl.num_programs(1) - 1)
    def _():
        sfin_ref[...] = st_s[...]


def _hgrn_scan(q, lf, k, i, g, gg, s0):
    b, l, d = q.shape
    t = _row_tile(l, SCAN_TILE)
    c = _row_tile(t, HGRN_CHUNK)
    blk = lambda bi, s: (bi, s, 0)
    st_shape = (HGRN_HEADS, HGRN_DK, HGRN_DK)
    return pl.pallas_call(
        functools.partial(_hgrn_scan_kernel, t, c),
        grid=(b, l // t),
        in_specs=[
            pl.BlockSpec((1, t, d), blk),
            pl.BlockSpec((1, t, d), blk),
            pl.BlockSpec((1, t, d), blk),
            pl.BlockSpec((1, t, d), blk),
            pl.BlockSpec((1, t, d), blk),
            _resident((1, d)),
            _resident((t, t)),
            _resident(st_shape),
        ],
        out_specs=[pl.BlockSpec((1, t, d), blk), pl.BlockSpec(st_shape, lambda bi, s: (0, 0, 0))],
        out_shape=[jax.ShapeDtypeStruct((b, l, d), BF16), jax.ShapeDtypeStruct(st_shape, F32)],
        scratch_shapes=([pltpu.VMEM(st_shape, F32)] + [pltpu.VMEM((t, d), F32)] * 2
                        + [pltpu.VMEM((t, d), BF16)] * 4 + [pltpu.VMEM((t, d), F32)] * 4),
        compiler_params=_params("arbitrary", "arbitrary"),
        name="hgrn_scan",
    )(q, lf, k, i, g, gg, _block_tril(t, c), s0)


def kernel(x, meta_tokens, attn_norm, ffn_norm, final_norm, fox_w_in, fox_b_f, fox_q_norm, fox_k_norm,
           fox_w_out, hgrn_w_in, hgrn_lower_bounds, hgrn_g_norm, hgrn_w_out, ffn_w_in, ffn_w_out):
    bsz, seq, d = x.shape
    row = lambda v: v.reshape(1, -1).astype(F32)

    order = jnp.argsort(fox_b_f[0])
    cols = (order[:, None] * FOX_HEAD_DIM + jnp.arange(FOX_HEAD_DIM)[None, :]).reshape(-1)
    fox_b_sorted = fox_b_f[0][order]

    w_fox = fox_w_in[0, :, :4 * d].reshape(d, 4, d)[:, :, cols].reshape(d, 4 * d).astype(BF16)
    w_f = jnp.pad(fox_w_in[0, :, 4 * d:][:, order], ((0, 0), (0, LANES - FOX_HEADS))).astype(BF16)
    w_fox_out = fox_w_out[0][cols, :].astype(BF16)
    w_hgrn = hgrn_w_in[0].astype(BF16)
    w_hgrn_out = hgrn_w_out[0].astype(BF16)
    w_ffn_in = ffn_w_in.astype(BF16)
    w_ffn_out = ffn_w_out.astype(BF16)
    head_of = np.arange(d) // FOX_HEAD_DIM
    bd = jnp.asarray(head_of[:, None] == np.arange(LANES)[None, :], BF16)
    ex = jnp.concatenate([bd.T, bd.T], axis=0)
    qg = row(jnp.tile(fox_q_norm[0], FOX_HEADS)) * (FOX_HEAD_DIM ** -0.5 * LOG2E)
    kg = row(jnp.tile(fox_k_norm[0], FOX_HEADS))
    bound2 = (1.01 * LOG2E * FOX_HEAD_DIM ** 0.5 * jnp.max(jnp.abs(fox_q_norm[0]))
              * jnp.max(jnp.abs(fox_k_norm[0]))).astype(BF16).astype(F32)
    flag = (bound2 <= BOUND_LIMIT).astype(jnp.int32).reshape(1)
    pq, pk = _placement(bound2)
    b_f = jnp.pad(row(fox_b_sorted), ((0, 0), (0, LANES - FOX_HEADS)))
    gg = row(jnp.tile(hgrn_g_norm[0], HGRN_HEADS))
    lbp = hgrn_lower_bounds.astype(F32)

    h_meta = meta_tokens.astype(F32)
    h_real = x.reshape(bsz * seq, d)
    seq3 = lambda a: a.reshape(bsz, seq, -1)

    proj = lambda h, n, c0: _fox_proj(h, n, row(attn_norm[0]), w_fox, w_f, b_f, c0, bd, ex, qg, kg, pq, pk)
    qm, km, vm, gm, c_meta = proj(h_meta, N_META, jnp.zeros((1, LANES), F32))
    qr, kr, vr, gr, c_tiles = proj(h_real, seq, c_meta[0])
    ym = _fox_attn(flag, jnp.zeros((FOX_PAIRS,), jnp.int32), qm[:, None], km[:, None], vm[None], gm[None])[0]
    pad_bias = np.zeros((FOX_HEADS, LANES, LANES), np.float32)
    for h in range(FOX_HEADS):
        pad_bias[h, N_META:, _bound_lane(h)] = PAD_KEY_BIAS
    km_pad = jnp.pad(km, ((0, 0), (0, LANES - N_META), (0, 0))) + jnp.asarray(pad_bias, BF16)
    vm_pad = jnp.pad(vm, ((0, LANES - N_META), (0, 0)))
    heads4 = lambda a: a.reshape(FOX_HEADS, bsz, seq, LANES)
    yr = _fox_attn(flag, _first_key_tile(c_tiles, c_meta[0], bsz), heads4(qr), heads4(kr), seq3(vr), seq3(gr),
                   meta=(km_pad, vm_pad)).reshape(bsz * seq, d)
    ffn0 = lambda y, h: _ffn(y, w_fox_out, h, row(ffn_norm[0]), w_ffn_in[0], w_ffn_out[0],
                             row(final_norm), False)
    h_meta = ffn0(ym, h_meta)
    h_real = ffn0(yr, h_real)

    proj = lambda h: _hgrn_proj(h, row(attn_norm[1]), w_hgrn, lbp)
    s0 = jnp.zeros((HGRN_HEADS, HGRN_DK, HGRN_DK), F32)
    _, s_meta = _hgrn_scan(*(a[None] for a in proj(h_meta)), gg, s0)
    yr, _ = _hgrn_scan(*(seq3(a) for a in proj(h_real)), gg, s_meta)
    out = _ffn(yr.reshape(bsz * seq, d), w_hgrn_out, h_real, row(ffn_norm[1]), w_ffn_in[1], w_ffn_out[1],
               row(final_norm), True)
    return out.reshape(bsz, seq, d)
```

```python
import functools
import math

import numpy as np
import jax
import jax.numpy as jnp
from jax import lax
from jax.experimental import pallas as pl
from jax.experimental.pallas import tpu as pltpu

D_MODEL = 1024
N_META = 16
FOX_HEADS = 16
FOX_HEAD_DIM = 64
HGRN_HEADS = 8
HGRN_DK = 128
HGRN_CHUNK = 64
FFN_HIDDEN = 2816
EPS = 1e-6

LANES = 128
FOX_PAIRS = FOX_HEADS // 2
FFN_TILE = 256
NEG_BIG = -1e30
PAD_KEY_BIAS = -30000.0
LOG2E = math.log2(math.e)
GUARD_BITS = 73.0
SKIP_BITS = 152.0
FAST_RANGE = 56.0
VMEM_LIMIT = 52 * 1024 * 1024

ROW_TILE = 512
ATTN_TILE = 512
ATTN_SUBTILES = 2
SCAN_TILE = 256

AUG_C = 0
AUG_ONE = 3
AUG_BOUND = 6
C3_MID, C3_LO, C3_ONE = 16, 32, 48

F32 = jnp.float32
BF16 = jnp.bfloat16

NN = (((1,), (0,)), ((), ()))
NT = (((1,), (1,)), ((), ()))
TN = (((0,), (0,)), ((), ()))


def _dot(a, b, dims=NN):
    return lax.dot_general(a, b, dims, preferred_element_type=F32)


def _params(*sem):
    return pltpu.CompilerParams(dimension_semantics=sem, vmem_limit_bytes=VMEM_LIMIT)


def _row_tile(m, pref):
    return pref if m % pref == 0 else m


def _resident(shape):
    return pl.BlockSpec(shape, lambda *_: (0,) * len(shape), pipeline_mode=pl.Buffered(1))


def _sigmoid(x):
    return 1.0 / (1.0 + jnp.exp(-x))


def _rms_rows(x, gain):
    ms = jnp.mean(x * x, axis=-1, keepdims=True)
    return x * lax.rsqrt(ms + EPS) * gain


def _split3(x):
    hi = x.astype(BF16)
    r1 = x - hi.astype(F32)
    mid = r1.astype(BF16)
    return hi, mid, (r1 - mid.astype(F32)).astype(BF16)


def _tri_cumsum(tri, x, pieces):
    if pieces == 3:
        hi, mid, lo = _split3(x)
        return _dot(tri, hi) + _dot(tri, mid) + _dot(tri, lo)
    hi = x.astype(BF16)
    return _dot(tri, hi) + _dot(tri, (x - hi.astype(F32)).astype(BF16))


def _block_tril(t, c):
    r = np.arange(t)
    return jnp.asarray((r[None, :] <= r[:, None]) & (r[None, :] // c == r[:, None] // c), BF16)


def _fox_proj_kernel(tiles_per_seq, x_ref, gain_ref, w_ref, wf_ref, bias_ref, c0_ref, tri_ref, bd_ref, ex_ref,
                     qg_ref, kg_ref, pq_ref, pk_ref, q_out, k_out, v_out, g_out, c_end, carry_s):
    @pl.when(lax.rem(pl.program_id(0), tiles_per_seq) == 0)
    def _():
        carry_s[...] = c0_ref[...]

    lane = lax.broadcasted_iota(jnp.int32, (1, LANES), 1)
    lo = lane < FOX_HEAD_DIM
    d = D_MODEL
    hr = tri_ref.shape[0]
    for r0 in range(0, x_ref.shape[0], hr):
        rs = slice(r0, r0 + hr)
        xn = _rms_rows(x_ref[rs, :], gain_ref[...]).astype(BF16)

        zz = _dot(xn, wf_ref[...]) + bias_ref[...]
        log_f = jnp.minimum(zz, 0.0) - jnp.log1p(jnp.exp(-jnp.abs(zz)))
        c = _tri_cumsum(tri_ref[...], log_f, 3) + carry_s[...]
        carry_s[...] = c[hr - 1:hr, :]

        cc = jnp.where(lane < FOX_HEADS, c * LOG2E, 0.0)
        hi, mid, low = _split3(cc)
        c3 = (hi.astype(F32) + pltpu.roll(mid.astype(F32), C3_MID, axis=1)
              + pltpu.roll(low.astype(F32), C3_LO, axis=1) + jnp.where(lane == C3_ONE, 1.0, 0.0)).astype(BF16)

        def head_slabs(acc, gain, place_ref, out):
            ss = _dot((acc * acc).astype(BF16), bd_ref[...])
            r = lax.rsqrt(ss * (1.0 / FOX_HEAD_DIM) + EPS)
            r_hi = r.astype(BF16)
            r_lo = (r - r_hi.astype(F32)).astype(BF16)
            xh = acc * _dot(jnp.concatenate([r_hi, r_lo], axis=1), ex_ref[...]) * gain
            aug = _dot(c3, place_ref[...])
            for p in range(FOX_PAIRS):
                pair = slice(p * LANES, (p + 1) * LANES)
                out[2 * p, rs, :] = jnp.where(lo, xh[:, pair], aug[:, pair]).astype(BF16)
                out[2 * p + 1, rs, :] = jnp.where(lo, aug[:, pair], xh[:, pair]).astype(BF16)

        head_slabs(_dot(xn, w_ref[:, 0:d]), qg_ref[...], pq_ref, q_out)
        head_slabs(_dot(xn, w_ref[:, d:2 * d]), kg_ref[...], pk_ref, k_out)
        v_out[rs, :] = _dot(xn, w_ref[:, 2 * d:3 * d]).astype(BF16)
        g_out[rs, :] = _dot(xn, w_ref[:, 3 * d:4 * d])
    c_end[0] = carry_s[...]


def _fox_proj(x, seq, gain, w, wf, bias, c0, bd, ex, qg, kg, pq, pk):
    m = x.shape[0]
    tm = _row_tile(seq, ROW_TILE)
    hr = tm // 2 if tm % 32 == 0 else tm
    d = D_MODEL
    row = lambda i: (i, 0)
    slab = pl.BlockSpec((FOX_HEADS, tm, LANES), lambda i: (0, i, 0))
    return pl.pallas_call(
        functools.partial(_fox_proj_kernel, seq // tm),
        grid=(m // tm,),
        in_specs=[
            pl.BlockSpec((tm, d), row),
            _resident((1, d)),
            _resident((d, 4 * d)),
            _resident((d, LANES)),
            _resident((1, LANES)),
            _resident((1, LANES)),
            _resident((hr, hr)),
            _resident((d, LANES)),
            _resident((2 * LANES, d)),
            _resident((1, d)),
            _resident((1, d)),
            _resident((LANES, d)),
            _resident((LANES, d)),
        ],
        out_specs=[slab, slab, pl.BlockSpec((tm, d), row), pl.BlockSpec((tm, d), row),
                   pl.BlockSpec((1, 1, LANES), lambda i: (i, 0, 0))],
        out_shape=[
            jax.ShapeDtypeStruct((FOX_HEADS, m, LANES), BF16),
            jax.ShapeDtypeStruct((FOX_HEADS, m, LANES), BF16),
            jax.ShapeDtypeStruct((m, d), BF16),
            jax.ShapeDtypeStruct((m, d), F32),
            jax.ShapeDtypeStruct((m // tm, 1, LANES), F32),
        ],
        scratch_shapes=[pltpu.VMEM((1, LANES), F32)],
        compiler_params=_params("arbitrary"),
        name="fox_proj",
    )(x, gain, w, wf, bias, c0, _block_tril(hr, hr), bd, ex, qg, kg, pq, pk)


def _bound_lane(h):
    return (FOX_HEAD_DIM if h % 2 == 0 else 0) + AUG_BOUND


def _placement(bound2):
    pq = np.zeros((LANES, D_MODEL), np.float32)
    pk = np.zeros((LANES, D_MODEL), np.float32)
    pb = np.zeros((LANES, D_MODEL), np.float32)
    for h in range(FOX_HEADS):
        base = (h // 2) * LANES + (FOX_HEAD_DIM if h % 2 == 0 else 0)
        for piece, src in enumerate((h, C3_MID + h, C3_LO + h)):
            pq[src, base + AUG_C + piece] = 1.0
            pk[src, base + AUG_ONE + piece] = -1.0
            pq[C3_ONE, base + AUG_ONE + piece] = 1.0
            pk[C3_ONE, base + AUG_C + piece] = 1.0
        pq[C3_ONE, base + AUG_BOUND] = 1.0
        pb[C3_ONE, base + AUG_BOUND] = 1.0
    return jnp.asarray(pq, BF16), (jnp.asarray(pk) - jnp.asarray(pb) * bound2).astype(BF16)


def _first_key_tile(c_tile_end, c_start, bsz, shift):
    nt = c_tile_end.shape[0] // bsz
    ce = c_tile_end.reshape(bsz, nt, LANES)[:, :, :FOX_HEADS] * LOG2E
    before = jnp.concatenate([jnp.broadcast_to(c_start[:, :FOX_HEADS] * LOG2E, (bsz, 1, FOX_HEADS)),
                              ce[:, :-1]], axis=1)
    gap = before[:, :, None, :] - ce[:, None, :, :]
    earlier = (jnp.arange(nt)[None, :] < jnp.arange(nt)[:, None])[None, :, :, None]
    dead = jnp.logical_and(gap - shift < -SKIP_BITS, earlier)
    dead = jnp.logical_and(dead[..., 0::2], dead[..., 1::2])
    first = jnp.sum(dead.astype(jnp.int32), axis=2)
    return first.transpose(0, 2, 1).reshape(-1)


def _fox_attn_kernel(has_meta, t, nt, lp, subs, flag_ref, first_ref, q_ref, k_ref, v_ref, gate_ref, *rest):
    if has_meta:
        km_ref, vm_ref, o_ref, m_s, l_s, lp_s, acc_s = rest
    else:
        o_ref, m_s, l_s, lp_s, acc_s = rest

    lane = lax.broadcasted_iota(jnp.int32, (1, LANES), 1)
    lo = lane < FOX_HEAD_DIM
    half = (lo, jnp.logical_not(lo))

    def run(online, sub):
        i = pl.program_id(2) * subs + sub
        rows = pl.ds(pl.multiple_of(sub * t, t), t)
        first = first_ref[(pl.program_id(0) * FOX_PAIRS + pl.program_id(1)) * nt + i]

        def tile(keys, vt, mask):
            alphas, pv = [], None
            zero_v = jnp.zeros_like(vt)
            for a in range(2):
                s = _dot(q_ref[a, 0, rows, :], keys(a), NT)
                if mask is not None:
                    s = jnp.where(mask, s, NEG_BIG)
                if online:
                    m_prev = m_s[a]
                    m_new = jnp.maximum(m_prev, jnp.max(s, axis=1, keepdims=True))
                    alpha = jnp.exp2(m_prev - m_new)
                    p = jnp.exp2(s - m_new)
                    l_s[a] = alpha * l_s[a] + jnp.sum(p, axis=1, keepdims=True)
                    m_s[a] = m_new
                    alphas.append(alpha)
                else:
                    p = jnp.exp2(s)
                    part = lp_s[a]
                    for c0 in range(0, p.shape[1], lp):
                        part = part + p[:, c0:c0 + lp]
                    lp_s[a] = part
                d = _dot(p.astype(BF16), jnp.where(half[a], vt, zero_v))
                pv = d if pv is None else pv + d
            if online:
                acc_s[...] = acc_s[...] * jnp.where(lo, alphas[0], alphas[1]) + pv
            else:
                acc_s[...] += pv

        def real_tile(off, size, mask):
            tile(lambda a: k_ref[a, 0, pl.ds(off, size), :], v_ref[0, pl.ds(off, size), :], mask)

        if online:
            m_s[...] = jnp.full(m_s.shape, NEG_BIG, F32)
            l_s[...] = jnp.zeros(l_s.shape, F32)
        else:
            lp_s[...] = jnp.zeros(lp_s.shape, F32)
        acc_s[...] = jnp.zeros(acc_s.shape, F32)

        if nt >= 2:
            lead = jnp.int32(0) if online else first
            n = i - lead

            @pl.when((n & 1) == 1)
            def _():
                real_tile(pl.multiple_of(lead * t, t), t, None)

            def body(j, carry):
                real_tile(pl.multiple_of((lead + (n & 1) + 2 * j) * t, t), 2 * t, None)
                return carry

            lax.fori_loop(0, lax.shift_right_logical(n, 1), body, 0)

        off = pl.multiple_of(i * t, t)

        def diag(n_pre):
            row = lax.broadcasted_iota(jnp.int32, (t, n_pre + t), 0)
            col = lax.broadcasted_iota(jnp.int32, (t, n_pre + t), 1)
            if n_pre:
                tile(lambda a: jnp.concatenate([km_ref[a], k_ref[a, 0, pl.ds(off, t), :]], axis=0),
                     jnp.concatenate([vm_ref[...], v_ref[0, pl.ds(off, t), :]], axis=0),
                     col - n_pre <= row)
            else:
                real_tile(off, t, col <= row)

        if has_meta and online:
            diag(LANES)
        elif has_meta:
            pl.when(first == 0)(lambda: diag(LANES))
            pl.when(first > 0)(lambda: diag(0))
        else:
            diag(0)

        if online:
            inv = jnp.where(lo, 1.0 / l_s[0], 1.0 / l_s[1])
        else:
            inv = jnp.where(lo, 1.0 / jnp.sum(lp_s[0], axis=1, keepdims=True),
                            1.0 / jnp.sum(lp_s[1], axis=1, keepdims=True))
        o_ref[0, rows, :] = (acc_s[...] * inv * _sigmoid(gate_ref[0, rows, :])).astype(BF16)

    def steps(online):
        if subs == 1:
            run(online, 0)
        else:
            def body(sub, carry):
                run(online, sub)
                return carry

            lax.fori_loop(0, subs, body, 0)

    @pl.when(flag_ref[0] == 1)
    def _():
        steps(False)

    @pl.when(flag_ref[0] != 1)
    def _():
        steps(True)


def _fox_attn(flag, first, q, k, v, gate, meta=None):
    b, l, d = v.shape
    t = _row_tile(l, ATTN_TILE)
    nt = l // t
    subs = ATTN_SUBTILES if nt % ATTN_SUBTILES == 0 else 1
    lp = LANES if t % LANES == 0 else t
    in_specs = [
        pl.BlockSpec(memory_space=pltpu.SMEM),
        pl.BlockSpec(memory_space=pltpu.SMEM),
        pl.BlockSpec((2, 1, subs * t, LANES), lambda bi, hp, i: (hp, bi, i, 0)),
        pl.BlockSpec((2, 1, l, LANES), lambda bi, hp, i: (hp, bi, 0, 0)),
        pl.BlockSpec((1, l, LANES), lambda bi, hp, i: (bi, 0, hp)),
        pl.BlockSpec((1, subs * t, LANES), lambda bi, hp, i: (bi, i, hp)),
    ]
    args = [flag, first, q, k, v, gate]
    if meta is not None:
        km, vm = meta
        in_specs += [
            pl.BlockSpec((2, LANES, LANES), lambda bi, hp, i: (hp, 0, 0)),
            pl.BlockSpec((LANES, LANES), lambda bi, hp, i: (0, hp)),
        ]
        args += [km, vm]
    return pl.pallas_call(
        functools.partial(_fox_attn_kernel, meta is not None, t, nt, lp, subs),
        grid=(b, FOX_PAIRS, nt // subs),
        in_specs=in_specs,
        out_specs=pl.BlockSpec((1, subs * t, LANES), lambda bi, hp, i: (bi, i, hp)),
        out_shape=jax.ShapeDtypeStruct((b, l, d), BF16),
        scratch_shapes=[
            pltpu.VMEM((2, t, 1), F32),
            pltpu.VMEM((2, t, 1), F32),
            pltpu.VMEM((2, t, lp), F32),
            pltpu.VMEM((t, LANES), F32),
        ],
        compiler_params=_params("arbitrary", "arbitrary", "arbitrary"),
        name="fox_attn",
    )(*args)


def _ffn_kernel(final, nh, y_ref, wmix_ref, h_ref, gain_ref, win_ref, wout_ref, fgain_ref, o_ref,
                xn_s, acc_s):
    hm = h_ref[...] + _dot(y_ref[...], wmix_ref[...])
    xn_s[...] = _rms_rows(hm, gain_ref[...]).astype(BF16)
    acc_s[...] = hm

    for j in range(nh):
        xn = xn_s[...]
        g = _dot(xn, win_ref[:, j * FFN_TILE:(j + 1) * FFN_TILE])
        u = _dot(xn, win_ref[:, FFN_HIDDEN + j * FFN_TILE:FFN_HIDDEN + (j + 1) * FFN_TILE])
        acc_s[...] += _dot((g * _sigmoid(g) * u).astype(BF16), wout_ref[j * FFN_TILE:(j + 1) * FFN_TILE, :])

    out = acc_s[...]
    if final:
        out = _rms_rows(out, fgain_ref[...])
    o_ref[...] = out


def _ffn(y, w_mix, h, gain, w_in, w_out, fgain, final):
    m, d = h.shape
    tm = _row_tile(m, ROW_TILE)
    nh = FFN_HIDDEN // FFN_TILE
    row = lambda i: (i, 0)
    return pl.pallas_call(
        functools.partial(_ffn_kernel, final, nh),
        grid=(m // tm,),
        in_specs=[
            pl.BlockSpec((tm, d), row),
            _resident((d, d)),
            pl.BlockSpec((tm, d), row),
            _resident((1, d)),
            _resident((d, 2 * FFN_HIDDEN)),
            _resident((FFN_HIDDEN, d)),
            _resident((1, d)),
        ],
        out_specs=pl.BlockSpec((tm, d), row),
        out_shape=jax.ShapeDtypeStruct((m, d), F32),
        scratch_shapes=[pltpu.VMEM((tm, d), BF16), pltpu.VMEM((tm, d), F32)],
        compiler_params=_params("arbitrary"),
        name="ffn",
    )(y, w_mix, h, gain, w_in, w_out, fgain)


def _hgrn_proj_kernel(x_ref, gain_ref, w_ref, lbp_ref, q_out, lf_out, k_out, v_out, g_out):
    d = D_MODEL

    lbp = lbp_ref[...]
    mx = jnp.maximum(lbp[0:1], lbp[1:2])
    e0 = jnp.exp(lbp[0:1] - mx)
    e1 = jnp.exp(lbp[1:2] - mx)
    den = e0 + e1
    sm0 = e0 / den
    lb = (sm0 + e1 / den) - sm0

    tm = x_ref.shape[0]
    hr = tm // 2 if tm % 32 == 0 else tm
    for r0 in range(0, tm, hr):
        rs = slice(r0, r0 + hr)
        xh = _rms_rows(x_ref[rs, :], gain_ref[...]).astype(BF16)
        qz = _dot(xh, w_ref[:, 0:d])
        q_out[rs, :] = (qz * _sigmoid(qz)).astype(BF16)
        sig = _sigmoid(_dot(xh, w_ref[:, d:2 * d]))
        lf_out[rs, :] = jnp.log2(lb + (1.0 - lb) * sig)
        k_out[rs, :] = (1.0 - lb) * (1.0 - sig)
        v_out[rs, :] = _dot(xh, w_ref[:, 2 * d:3 * d]).astype(BF16)
        gz = _dot(xh, w_ref[:, 3 * d:4 * d])
        g_out[rs, :] = (gz * _sigmoid(gz)).astype(BF16)


def _hgrn_proj(x, gain, w, lbp):
    m, d = x.shape
    tm = _row_tile(m, ROW_TILE)
    row = lambda i: (i, 0)
    return pl.pallas_call(
        _hgrn_proj_kernel,
        grid=(m // tm,),
        in_specs=[pl.BlockSpec((tm, d), row), _resident((1, d)), _resident((d, 4 * d)), _resident((2, d))],
        out_specs=[pl.BlockSpec((tm, d), row)] * 5,
        out_shape=[jax.ShapeDtypeStruct((m, d), BF16), jax.ShapeDtypeStruct((m, d), F32),
                   jax.ShapeDtypeStruct((m, d), F32), jax.ShapeDtypeStruct((m, d), BF16),
                   jax.ShapeDtypeStruct((m, d), BF16)],
        compiler_params=_params("arbitrary"),
        name="hgrn_proj",
    )(x, gain, w, lbp)


def _hgrn_scan_kernel(t, c, q_ref, lf_ref, k_ref, i_ref, g_ref, gg_ref, tri_ref, s0_ref,
                      y_ref, sfin_ref, st_s, oi_s, oa_s, qin_s, kout_s, qf_s, kf_s, b_s, q_s, k_s, v_s):
    step = pl.program_id(1)
    nc = t // c
    mid = c // 2
    heads = [slice(h * HGRN_DK, (h + 1) * HGRN_DK) for h in range(HGRN_HEADS)]

    @pl.when(step == 0)
    def _():
        st_s[...] = s0_ref[...]

    qq = q_ref[0].astype(F32)
    kk = k_ref[0]
    v_b = i_ref[0]

    b = _tri_cumsum(tri_ref[...], lf_ref[0], 2)

    spread = jnp.float32(0.0)
    decay = []
    for ci in range(nc):
        rs = slice(ci * c, (ci + 1) * c)
        bc = b[rs]
        b_last = bc[c - 1:c]
        b_mid = bc[mid - 1:mid]
        spread = jnp.maximum(spread, jnp.maximum(jnp.max(bc[0:1] - b_mid), jnp.max(b_mid - b_last)))
        q_in = qq[rs] * jnp.exp2(bc)
        k_out = kk[rs] * jnp.exp2(b_last - bc)
        qin_s[rs] = q_in.astype(BF16)
        kout_s[rs] = k_out.astype(BF16)
        qf_s[rs] = (q_in * jnp.exp2(-b_mid)).astype(BF16)
        kf_s[rs] = (k_out * jnp.exp2(b_mid - b_last)).astype(BF16)
        decay.append(jnp.exp2(b_last))

    keep = tri_ref[...] > 0
    for sl in heads:
        a = jnp.where(keep, _dot(qf_s[:, sl], kf_s[:, sl], NT), 0.0)
        oa_s[:, sl] = _dot(a.astype(BF16), v_b[:, sl])

    for ci in range(nc):
        rs = slice(ci * c, (ci + 1) * c)
        for h, sl in enumerate(heads):
            st = st_s[h]
            oi_s[rs, sl] = _dot(qin_s[rs, sl], st.astype(BF16), NT)
            st_s[h] = st * decay[ci][:, sl] + _dot(v_b[rs, sl], kout_s[rs, sl], TN)

    @pl.when(spread > FAST_RANGE)
    def _():
        b_s[...] = b
        k_s[...] = kk
        q_s[...] = qq
        v_s[...] = v_b.astype(F32)
        oa_s[...] = jnp.zeros(oa_s.shape, F32)
        rowc = lax.broadcasted_iota(jnp.int32, (c, 1), 0)

        def body(s, carry):
            base = pl.multiple_of((s // c) * c, c)
            blk = pl.ds(base, c)
            p = (q_s[blk, :] * jnp.exp2(jnp.minimum(b_s[blk, :] - b_s[pl.ds(s, 1), :], 0.0))
                 * k_s[pl.ds(s, 1), :])
            vs = v_s[pl.ds(s, 1), :]
            seen = rowc + base >= s
            for sl in heads:
                w = jnp.where(seen, jnp.sum(p[:, sl], axis=1, keepdims=True), 0.0)
                oa_s[blk, sl] += w * vs[:, sl]
            return carry

        lax.fori_loop(0, t, body, 0)

    gate = g_ref[0].astype(F32)
    o = oi_s[...] + oa_s[...]
    for sl in heads:
        y_ref[0, :, sl] = (_rms_rows(o[:, sl], gg_ref[:, sl]) * gate[:, sl]).astype(BF16)

    @pl.when(step == pl.num_programs(1) - 1)
    def _():
        sfin_ref[...] = st_s[...]


def _hgrn_scan(q, lf, k, i, g, gg, s0):
    b, l, d = q.shape
    t = _row_tile(l, SCAN_TILE)
    c = _row_tile(t, HGRN_CHUNK)
    blk = lambda bi, s: (bi, s, 0)
    st_shape = (HGRN_HEADS, HGRN_DK, HGRN_DK)
    return pl.pallas_call(
        functools.partial(_hgrn_scan_kernel, t, c),
        grid=(b, l // t),
        in_specs=[
            pl.BlockSpec((1, t, d), blk),
            pl.BlockSpec((1, t, d), blk),
            pl.BlockSpec((1, t, d), blk),
            pl.BlockSpec((1, t, d), blk),
            pl.BlockSpec((1, t, d), blk),
            _resident((1, d)),
            _resident((t, t)),
            _resident(st_shape),
        ],
        out_specs=[pl.BlockSpec((1, t, d), blk), pl.BlockSpec(st_shape, lambda bi, s: (0, 0, 0))],
        out_shape=[jax.ShapeDtypeStruct((b, l, d), BF16), jax.ShapeDtypeStruct(st_shape, F32)],
        scratch_shapes=([pltpu.VMEM(st_shape, F32)] + [pltpu.VMEM((t, d), F32)] * 2
                        + [pltpu.VMEM((t, d), BF16)] * 4 + [pltpu.VMEM((t, d), F32)] * 4),
        compiler_params=_params("arbitrary", "arbitrary"),
        name="hgrn_scan",
    )(q, lf, k, i, g, gg, _block_tril(t, c), s0)


def kernel(x, meta_tokens, attn_norm, ffn_norm, final_norm, fox_w_in, fox_b_f, fox_q_norm, fox_k_norm,
           fox_w_out, hgrn_w_in, hgrn_lower_bounds, hgrn_g_norm, hgrn_w_out, ffn_w_in, ffn_w_out):
    bsz, seq, d = x.shape
    row = lambda v: v.reshape(1, -1).astype(F32)

    order = jnp.argsort(fox_b_f[0])
    cols = (order[:, None] * FOX_HEAD_DIM + jnp.arange(FOX_HEAD_DIM)[None, :]).reshape(-1)
    fox_b_sorted = fox_b_f[0][order]

    w_fox = fox_w_in[0, :, :4 * d].reshape(d, 4, d)[:, :, cols].reshape(d, 4 * d).astype(BF16)
    w_f = jnp.pad(fox_w_in[0, :, 4 * d:][:, order], ((0, 0), (0, LANES - FOX_HEADS))).astype(BF16)
    w_fox_out = fox_w_out[0][cols, :].astype(BF16)
    w_hgrn = hgrn_w_in[0].astype(BF16)
    w_hgrn_out = hgrn_w_out[0].astype(BF16)
    w_ffn_in = ffn_w_in.astype(BF16)
    w_ffn_out = ffn_w_out.astype(BF16)
    head_of = np.arange(d) // FOX_HEAD_DIM
    bd = jnp.asarray(head_of[:, None] == np.arange(LANES)[None, :], BF16)
    ex = jnp.concatenate([bd.T, bd.T], axis=0)
    qg = row(jnp.tile(fox_q_norm[0], FOX_HEADS)) * (FOX_HEAD_DIM ** -0.5 * LOG2E)
    kg = row(jnp.tile(fox_k_norm[0], FOX_HEADS))
    bound2 = (1.01 * LOG2E * FOX_HEAD_DIM ** 0.5 * jnp.max(jnp.abs(fox_q_norm[0]))
              * jnp.max(jnp.abs(fox_k_norm[0]))).astype(BF16).astype(F32)
    flag = (2.0 * bound2 <= GUARD_BITS).astype(jnp.int32).reshape(1)
    shifted = (bound2 + jnp.maximum(GUARD_BITS - 2.0 * bound2, 0.0)).astype(BF16).astype(F32)
    pq, pk = _placement(shifted)
    b_f = jnp.pad(row(fox_b_sorted), ((0, 0), (0, LANES - FOX_HEADS)))
    gg = row(jnp.tile(hgrn_g_norm[0], HGRN_HEADS))
    lbp = hgrn_lower_bounds.astype(F32)

    h_meta = meta_tokens.astype(F32)
    h_real = x.reshape(bsz * seq, d)
    seq3 = lambda a: a.reshape(bsz, seq, -1)

    proj = lambda h, n, c0: _fox_proj(h, n, row(attn_norm[0]), w_fox, w_f, b_f, c0, bd, ex, qg, kg, pq, pk)
    qm, km, vm, gm, c_meta = proj(h_meta, N_META, jnp.zeros((1, LANES), F32))
    qr, kr, vr, gr, c_tiles = proj(h_real, seq, c_meta[0])
    ym = _fox_attn(flag, jnp.zeros((FOX_PAIRS,), jnp.int32), qm[:, None], km[:, None], vm[None], gm[None])[0]
    pad_bias = np.zeros((FOX_HEADS, LANES, LANES), np.float32)
    for h in range(FOX_HEADS):
        pad_bias[h, N_META:, _bound_lane(h)] = PAD_KEY_BIAS
    km_pad = jnp.pad(km, ((0, 0), (0, LANES - N_META), (0, 0))) + jnp.asarray(pad_bias, BF16)
    vm_pad = jnp.pad(vm, ((0, LANES - N_META), (0, 0)))
    heads4 = lambda a: a.reshape(FOX_HEADS, bsz, seq, LANES)
    yr = _fox_attn(flag, _first_key_tile(c_tiles, c_meta[0], bsz, shifted - bound2), heads4(qr), heads4(kr), seq3(vr), seq3(gr),
                   meta=(km_pad, vm_pad)).reshape(bsz * seq, d)
    ffn0 = lambda y, h: _ffn(y, w_fox_out, h, row(ffn_norm[0]), w_ffn_in[0], w_ffn_out[0],
                             row(final_norm), False)
    h_meta = ffn0(ym, h_meta)
    h_real = ffn0(yr, h_real)

    proj = lambda h: _hgrn_proj(h, row(attn_norm[1]), w_hgrn, lbp)
    s0 = jnp.zeros((HGRN_HEADS, HGRN_DK, HGRN_DK), F32)
    _, s_meta = _hgrn_scan(*(a[None] for a in proj(h_meta)), gg, s0)
    yr, _ = _hgrn_scan(*(seq3(a) for a in proj(h_real)), gg, s_meta)
    out = _ffn(yr.reshape(bsz * seq, d), w_hgrn_out, h_real, row(ffn_norm[1]), w_ffn_in[1], w_ffn_out[1],
               row(final_norm), True)
    return out.reshape(bsz, seq, d)
```

```python
import functools
import math

import numpy as np
import jax
import jax.numpy as jnp
from jax import lax
from jax.experimental import pallas as pl
from jax.experimental.pallas import tpu as pltpu

D_MODEL = 1024
N_META = 16
FOX_HEADS = 16
FOX_HEAD_DIM = 64
HGRN_HEADS = 8
HGRN_DK = 128
HGRN_CHUNK = 64
FFN_HIDDEN = 2816
EPS = 1e-6

LANES = 128
FOX_PAIRS = FOX_HEADS // 2
FFN_TILE = 256
NEG_BIG = -1e30
PAD_KEY_BIAS = -30000.0
LOG2E = math.log2(math.e)
GUARD_BITS = 73.0
SKIP_BITS = 152.0
FAST_RANGE = 56.0
VMEM_LIMIT = 52 * 1024 * 1024

ROW_TILE = 512
ATTN_TILE = 512
SCAN_TILE = 256

AUG_C = 0
AUG_ONE = 3
AUG_BOUND = 6
C3_MID, C3_LO, C3_ONE = 16, 32, 48

F32 = jnp.float32
BF16 = jnp.bfloat16

NN = (((1,), (0,)), ((), ()))
NT = (((1,), (1,)), ((), ()))
TN = (((0,), (0,)), ((), ()))


def _dot(a, b, dims=NN):
    return lax.dot_general(a, b, dims, preferred_element_type=F32)


def _params(*sem):
    return pltpu.CompilerParams(dimension_semantics=sem, vmem_limit_bytes=VMEM_LIMIT)


def _row_tile(m, pref):
    return pref if m % pref == 0 else m


def _resident(shape):
    return pl.BlockSpec(shape, lambda *_: (0,) * len(shape), pipeline_mode=pl.Buffered(1))


def _sigmoid(x):
    return 1.0 / (1.0 + jnp.exp(-x))


def _rms_rows(x, gain):
    ms = jnp.mean(x * x, axis=-1, keepdims=True)
    return x * lax.rsqrt(ms + EPS) * gain


def _split3(x):
    hi = x.astype(BF16)
    r1 = x - hi.astype(F32)
    mid = r1.astype(BF16)
    return hi, mid, (r1 - mid.astype(F32)).astype(BF16)


def _tri_cumsum(tri, x, pieces):
    if pieces == 3:
        hi, mid, lo = _split3(x)
        return _dot(tri, hi) + _dot(tri, mid) + _dot(tri, lo)
    hi = x.astype(BF16)
    return _dot(tri, hi) + _dot(tri, (x - hi.astype(F32)).astype(BF16))


def _block_tril(t, c):
    r = np.arange(t)
    return jnp.asarray((r[None, :] <= r[:, None]) & (r[None, :] // c == r[:, None] // c), BF16)


def _fox_proj_kernel(tiles_per_seq, x_ref, gain_ref, w_ref, wf_ref, bias_ref, c0_ref, tri_ref, bd_ref, ex_ref,
                     qg_ref, kg_ref, pq_ref, pk_ref, q_out, k_out, v_out, g_out, c_end, carry_s):
    @pl.when(lax.rem(pl.program_id(0), tiles_per_seq) == 0)
    def _():
        carry_s[...] = c0_ref[...]

    xn = _rms_rows(x_ref[...], gain_ref[...]).astype(BF16)
    lane = lax.broadcasted_iota(jnp.int32, (1, LANES), 1)
    lo = lane < FOX_HEAD_DIM

    zz = _dot(xn, wf_ref[...]) + bias_ref[...]
    log_f = jnp.minimum(zz, 0.0) - jnp.log1p(jnp.exp(-jnp.abs(zz)))
    c = _tri_cumsum(tri_ref[...], log_f, 3) + carry_s[...]
    tm = c.shape[0]
    carry_s[...] = c[tm - 1:tm, :]
    c_end[0] = c[tm - 1:tm, :]

    cc = jnp.where(lane < FOX_HEADS, c * LOG2E, 0.0)
    hi, mid, low = _split3(cc)
    c3 = (hi.astype(F32) + pltpu.roll(mid.astype(F32), C3_MID, axis=1)
          + pltpu.roll(low.astype(F32), C3_LO, axis=1) + jnp.where(lane == C3_ONE, 1.0, 0.0)).astype(BF16)

    def head_slabs(acc, gain, place_ref, out):
        ss = _dot((acc * acc).astype(BF16), bd_ref[...])
        r = lax.rsqrt(ss * (1.0 / FOX_HEAD_DIM) + EPS)
        r_hi = r.astype(BF16)
        r_lo = (r - r_hi.astype(F32)).astype(BF16)
        xh = acc * _dot(jnp.concatenate([r_hi, r_lo], axis=1), ex_ref[...]) * gain
        aug = _dot(c3, place_ref[...])
        for p in range(FOX_PAIRS):
            pair = slice(p * LANES, (p + 1) * LANES)
            out[2 * p] = jnp.where(lo, xh[:, pair], aug[:, pair]).astype(BF16)
            out[2 * p + 1] = jnp.where(lo, aug[:, pair], xh[:, pair]).astype(BF16)

    d = D_MODEL
    head_slabs(_dot(xn, w_ref[:, 0:d]), qg_ref[...], pq_ref, q_out)
    head_slabs(_dot(xn, w_ref[:, d:2 * d]), kg_ref[...], pk_ref, k_out)
    v_out[...] = _dot(xn, w_ref[:, 2 * d:3 * d]).astype(BF16)
    g_out[...] = _dot(xn, w_ref[:, 3 * d:4 * d])


def _fox_proj(x, seq, gain, w, wf, bias, c0, bd, ex, qg, kg, pq, pk):
    m = x.shape[0]
    tm = _row_tile(seq, ROW_TILE)
    d = D_MODEL
    row = lambda i: (i, 0)
    slab = pl.BlockSpec((FOX_HEADS, tm, LANES), lambda i: (0, i, 0))
    return pl.pallas_call(
        functools.partial(_fox_proj_kernel, seq // tm),
        grid=(m // tm,),
        in_specs=[
            pl.BlockSpec((tm, d), row),
            _resident((1, d)),
            _resident((d, 4 * d)),
            _resident((d, LANES)),
            _resident((1, LANES)),
            _resident((1, LANES)),
            _resident((tm, tm)),
            _resident((d, LANES)),
            _resident((2 * LANES, d)),
            _resident((1, d)),
            _resident((1, d)),
            _resident((LANES, d)),
            _resident((LANES, d)),
        ],
        out_specs=[slab, slab, pl.BlockSpec((tm, d), row), pl.BlockSpec((tm, d), row),
                   pl.BlockSpec((1, 1, LANES), lambda i: (i, 0, 0))],
        out_shape=[
            jax.ShapeDtypeStruct((FOX_HEADS, m, LANES), BF16),
            jax.ShapeDtypeStruct((FOX_HEADS, m, LANES), BF16),
            jax.ShapeDtypeStruct((m, d), BF16),
            jax.ShapeDtypeStruct((m, d), F32),
            jax.ShapeDtypeStruct((m // tm, 1, LANES), F32),
        ],
        scratch_shapes=[pltpu.VMEM((1, LANES), F32)],
        compiler_params=_params("arbitrary"),
        name="fox_proj",
    )(x, gain, w, wf, bias, c0, _block_tril(tm, tm), bd, ex, qg, kg, pq, pk)


def _bound_lane(h):
    return (FOX_HEAD_DIM if h % 2 == 0 else 0) + AUG_BOUND


def _placement(bound2):
    pq = np.zeros((LANES, D_MODEL), np.float32)
    pk = np.zeros((LANES, D_MODEL), np.float32)
    pb = np.zeros((LANES, D_MODEL), np.float32)
    for h in range(FOX_HEADS):
        base = (h // 2) * LANES + (FOX_HEAD_DIM if h % 2 == 0 else 0)
        for piece, src in enumerate((h, C3_MID + h, C3_LO + h)):
            pq[src, base + AUG_C + piece] = 1.0
            pk[src, base + AUG_ONE + piece] = -1.0
            pq[C3_ONE, base + AUG_ONE + piece] = 1.0
            pk[C3_ONE, base + AUG_C + piece] = 1.0
        pq[C3_ONE, base + AUG_BOUND] = 1.0
        pb[C3_ONE, base + AUG_BOUND] = 1.0
    return jnp.asarray(pq, BF16), (jnp.asarray(pk) - jnp.asarray(pb) * bound2).astype(BF16)


def _first_key_tile(c_tile_end, c_start, bsz, shift):
    nt = c_tile_end.shape[0] // bsz
    ce = c_tile_end.reshape(bsz, nt, LANES)[:, :, :FOX_HEADS] * LOG2E
    before = jnp.concatenate([jnp.broadcast_to(c_start[:, :FOX_HEADS] * LOG2E, (bsz, 1, FOX_HEADS)),
                              ce[:, :-1]], axis=1)
    gap = before[:, :, None, :] - ce[:, None, :, :]
    earlier = (jnp.arange(nt)[None, :] < jnp.arange(nt)[:, None])[None, :, :, None]
    dead = jnp.logical_and(gap - shift < -SKIP_BITS, earlier)
    dead = jnp.logical_and(dead[..., 0::2], dead[..., 1::2])
    first = jnp.sum(dead.astype(jnp.int32), axis=2)
    return first.transpose(0, 2, 1).reshape(-1)


def _fox_attn_kernel(has_meta, t, nt, lp, flag_ref, first_ref, q_ref, k_ref, v_ref, gate_ref, *rest):
    if has_meta:
        km_ref, vm_ref, o_ref, m_s, l_s, lp_s, acc_s = rest
    else:
        o_ref, m_s, l_s, lp_s, acc_s = rest

    lane = lax.broadcasted_iota(jnp.int32, (1, LANES), 1)
    lo = lane < FOX_HEAD_DIM
    half = (lo, jnp.logical_not(lo))

    def run(online):
        i = pl.program_id(2)
        first = first_ref[(pl.program_id(0) * FOX_PAIRS + pl.program_id(1)) * nt + i]

        def tile(keys, vt, mask):
            alphas, pv = [], None
            zero_v = jnp.zeros_like(vt)
            for a in range(2):
                s = _dot(q_ref[a, 0], keys(a), NT)
                if mask is not None:
                    s = jnp.where(mask, s, NEG_BIG)
                if online:
                    m_prev = m_s[a]
                    m_new = jnp.maximum(m_prev, jnp.max(s, axis=1, keepdims=True))
                    alpha = jnp.exp2(m_prev - m_new)
                    p = jnp.exp2(s - m_new)
                    l_s[a] = alpha * l_s[a] + jnp.sum(p, axis=1, keepdims=True)
                    m_s[a] = m_new
                    alphas.append(alpha)
                else:
                    p = jnp.exp2(s)
                    part = lp_s[a]
                    for c0 in range(0, p.shape[1], lp):
                        part = part + p[:, c0:c0 + lp]
                    lp_s[a] = part
                d = _dot(p.astype(BF16), jnp.where(half[a], vt, zero_v))
                pv = d if pv is None else pv + d
            if online:
                acc_s[...] = acc_s[...] * jnp.where(lo, alphas[0], alphas[1]) + pv
            else:
                acc_s[...] += pv

        def real_tile(off, size, mask):
            tile(lambda a: k_ref[a, 0, pl.ds(off, size), :], v_ref[0, pl.ds(off, size), :], mask)

        if online:
            m_s[...] = jnp.full(m_s.shape, NEG_BIG, F32)
            l_s[...] = jnp.zeros(l_s.shape, F32)
        else:
            lp_s[...] = jnp.zeros(lp_s.shape, F32)
        acc_s[...] = jnp.zeros(acc_s.shape, F32)

        if nt >= 2:
            lead = jnp.int32(0) if online else first
            n = i - lead

            @pl.when((n & 1) == 1)
            def _():
                real_tile(pl.multiple_of(lead * t, t), t, None)

            def body(j, carry):
                real_tile(pl.multiple_of((lead + (n & 1) + 2 * j) * t, t), 2 * t, None)
                return carry

            lax.fori_loop(0, lax.shift_right_logical(n, 1), body, 0)

        off = pl.multiple_of(i * t, t)

        def diag(n_pre):
            row = lax.broadcasted_iota(jnp.int32, (t, n_pre + t), 0)
            col = lax.broadcasted_iota(jnp.int32, (t, n_pre + t), 1)
            if n_pre:
                tile(lambda a: jnp.concatenate([km_ref[a], k_ref[a, 0, pl.ds(off, t), :]], axis=0),
                     jnp.concatenate([vm_ref[...], v_ref[0, pl.ds(off, t), :]], axis=0),
                     col - n_pre <= row)
            else:
                real_tile(off, t, col <= row)

        if has_meta and online:
            diag(LANES)
        elif has_meta:
            pl.when(first == 0)(lambda: diag(LANES))
            pl.when(first > 0)(lambda: diag(0))
        else:
            diag(0)

        if online:
            inv = jnp.where(lo, 1.0 / l_s[0], 1.0 / l_s[1])
        else:
            inv = jnp.where(lo, 1.0 / jnp.sum(lp_s[0], axis=1, keepdims=True),
                            1.0 / jnp.sum(lp_s[1], axis=1, keepdims=True))
        o_ref[0] = (acc_s[...] * inv * _sigmoid(gate_ref[0])).astype(BF16)

    @pl.when(flag_ref[0] == 1)
    def _():
        run(False)

    @pl.when(flag_ref[0] != 1)
    def _():
        run(True)


def _fox_attn(flag, first, q, k, v, gate, meta=None):
    b, l, d = v.shape
    t = _row_tile(l, ATTN_TILE)
    nt = l // t
    lp = LANES if t % LANES == 0 else t
    in_specs = [
        pl.BlockSpec(memory_space=pltpu.SMEM),
        pl.BlockSpec(memory_space=pltpu.SMEM),
        pl.BlockSpec((2, 1, t, LANES), lambda bi, hp, i: (hp, bi, i, 0)),
        pl.BlockSpec((2, 1, l, LANES), lambda bi, hp, i: (hp, bi, 0, 0)),
        pl.BlockSpec((1, l, LANES), lambda bi, hp, i: (bi, 0, hp)),
        pl.BlockSpec((1, t, LANES), lambda bi, hp, i: (bi, i, hp)),
    ]
    args = [flag, first, q, k, v, gate]
    if meta is not None:
        km, vm = meta
        in_specs += [
            pl.BlockSpec((2, LANES, LANES), lambda bi, hp, i: (hp, 0, 0)),
            pl.BlockSpec((LANES, LANES), lambda bi, hp, i: (0, hp)),
        ]
        args += [km, vm]
    return pl.pallas_call(
        functools.partial(_fox_attn_kernel, meta is not None, t, nt, lp),
        grid=(b, FOX_PAIRS, nt),
        in_specs=in_specs,
        out_specs=pl.BlockSpec((1, t, LANES), lambda bi, hp, i: (bi, i, hp)),
        out_shape=jax.ShapeDtypeStruct((b, l, d), BF16),
        scratch_shapes=[
            pltpu.VMEM((2, t, 1), F32),
            pltpu.VMEM((2, t, 1), F32),
            pltpu.VMEM((2, t, lp), F32),
            pltpu.VMEM((t, LANES), F32),
        ],
        compiler_params=_params("arbitrary", "arbitrary", "arbitrary"),
        name="fox_attn",
    )(*args)


def _ffn_kernel(final, nh, y_ref, wmix_ref, h_ref, gain_ref, win_ref, wout_ref, fgain_ref, o_ref,
                xn_s, acc_s):
    hm = h_ref[...] + _dot(y_ref[...], wmix_ref[...])
    xn_s[...] = _rms_rows(hm, gain_ref[...]).astype(BF16)
    acc_s[...] = hm

    for j in range(nh):
        xn = xn_s[...]
        g = _dot(xn, win_ref[:, j * FFN_TILE:(j + 1) * FFN_TILE])
        u = _dot(xn, win_ref[:, FFN_HIDDEN + j * FFN_TILE:FFN_HIDDEN + (j + 1) * FFN_TILE])
        acc_s[...] += _dot((g * _sigmoid(g) * u).astype(BF16), wout_ref[j * FFN_TILE:(j + 1) * FFN_TILE, :])

    out = acc_s[...]
    if final:
        out = _rms_rows(out, fgain_ref[...])
    o_ref[...] = out


def _ffn(y, w_mix, h, gain, w_in, w_out, fgain, final):
    m, d = h.shape
    tm = _row_tile(m, ROW_TILE)
    nh = FFN_HIDDEN // FFN_TILE
    row = lambda i: (i, 0)
    return pl.pallas_call(
        functools.partial(_ffn_kernel, final, nh),
        grid=(m // tm,),
        in_specs=[
            pl.BlockSpec((tm, d), row),
            _resident((d, d)),
            pl.BlockSpec((tm, d), row),
            _resident((1, d)),
            _resident((d, 2 * FFN_HIDDEN)),
            _resident((FFN_HIDDEN, d)),
            _resident((1, d)),
        ],
        out_specs=pl.BlockSpec((tm, d), row),
        out_shape=jax.ShapeDtypeStruct((m, d), F32),
        scratch_shapes=[pltpu.VMEM((tm, d), BF16), pltpu.VMEM((tm, d), F32)],
        compiler_params=_params("arbitrary"),
        name="ffn",
    )(y, w_mix, h, gain, w_in, w_out, fgain)


def _hgrn_proj_kernel(x_ref, gain_ref, w_ref, lbp_ref, q_out, lf_out, k_out, v_out, g_out):
    d = D_MODEL

    lbp = lbp_ref[...]
    mx = jnp.maximum(lbp[0:1], lbp[1:2])
    e0 = jnp.exp(lbp[0:1] - mx)
    e1 = jnp.exp(lbp[1:2] - mx)
    den = e0 + e1
    sm0 = e0 / den
    lb = (sm0 + e1 / den) - sm0

    tm = x_ref.shape[0]
    hr = tm // 2 if tm % 32 == 0 else tm
    for r0 in range(0, tm, hr):
        rs = slice(r0, r0 + hr)
        xh = _rms_rows(x_ref[rs, :], gain_ref[...]).astype(BF16)
        qz = _dot(xh, w_ref[:, 0:d])
        q_out[rs, :] = (qz * _sigmoid(qz)).astype(BF16)
        sig = _sigmoid(_dot(xh, w_ref[:, d:2 * d]))
        lf_out[rs, :] = jnp.log2(lb + (1.0 - lb) * sig)
        k_out[rs, :] = (1.0 - lb) * (1.0 - sig)
        v_out[rs, :] = _dot(xh, w_ref[:, 2 * d:3 * d]).astype(BF16)
        gz = _dot(xh, w_ref[:, 3 * d:4 * d])
        g_out[rs, :] = (gz * _sigmoid(gz)).astype(BF16)


def _hgrn_proj(x, gain, w, lbp):
    m, d = x.shape
    tm = _row_tile(m, ROW_TILE)
    row = lambda i: (i, 0)
    return pl.pallas_call(
        _hgrn_proj_kernel,
        grid=(m // tm,),
        in_specs=[pl.BlockSpec((tm, d), row), _resident((1, d)), _resident((d, 4 * d)), _resident((2, d))],
        out_specs=[pl.BlockSpec((tm, d), row)] * 5,
        out_shape=[jax.ShapeDtypeStruct((m, d), BF16), jax.ShapeDtypeStruct((m, d), F32),
                   jax.ShapeDtypeStruct((m, d), F32), jax.ShapeDtypeStruct((m, d), BF16),
                   jax.ShapeDtypeStruct((m, d), BF16)],
        compiler_params=_params("arbitrary"),
        name="hgrn_proj",
    )(x, gain, w, lbp)


def _hgrn_scan_kernel(t, c, q_ref, lf_ref, k_ref, i_ref, g_ref, gg_ref, tri_ref, s0_ref,
                      y_ref, sfin_ref, st_s, oi_s, oa_s, qin_s, kout_s, qf_s, kf_s, b_s, q_s, k_s, v_s):
    step = pl.program_id(1)
    nc = t // c
    mid = c // 2
    heads = [slice(h * HGRN_DK, (h + 1) * HGRN_DK) for h in range(HGRN_HEADS)]

    @pl.when(step == 0)
    def _():
        st_s[...] = s0_ref[...]

    qq = q_ref[0].astype(F32)
    kk = k_ref[0]
    v_b = i_ref[0]

    b = _tri_cumsum(tri_ref[...], lf_ref[0], 2)

    spread = jnp.float32(0.0)
    decay = []
    for ci in range(nc):
        rs = slice(ci * c, (ci + 1) * c)
        bc = b[rs]
        b_last = bc[c - 1:c]
        b_mid = bc[mid - 1:mid]
        spread = jnp.maximum(spread, jnp.maximum(jnp.max(bc[0:1] - b_mid), jnp.max(b_mid - b_last)))
        q_in = qq[rs] * jnp.exp2(bc)
        k_out = kk[rs] * jnp.exp2(b_last - bc)
        qin_s[rs] = q_in.astype(BF16)
        kout_s[rs] = k_out.astype(BF16)
        qf_s[rs] = (q_in * jnp.exp2(-b_mid)).astype(BF16)
        kf_s[rs] = (k_out * jnp.exp2(b_mid - b_last)).astype(BF16)
        decay.append(jnp.exp2(b_last))

    keep = tri_ref[...] > 0
    for sl in heads:
        a = jnp.where(keep, _dot(qf_s[:, sl], kf_s[:, sl], NT), 0.0)
        oa_s[:, sl] = _dot(a.astype(BF16), v_b[:, sl])

    for ci in range(nc):
        rs = slice(ci * c, (ci + 1) * c)
        for h, sl in enumerate(heads):
            st = st_s[h]
            oi_s[rs, sl] = _dot(qin_s[rs, sl], st.astype(BF16), NT)
            st_s[h] = st * decay[ci][:, sl] + _dot(v_b[rs, sl], kout_s[rs, sl], TN)

    @pl.when(spread > FAST_RANGE)
    def _():
        b_s[...] = b
        k_s[...] = kk
        q_s[...] = qq
        v_s[...] = v_b.astype(F32)
        oa_s[...] = jnp.zeros(oa_s.shape, F32)
        rowc = lax.broadcasted_iota(jnp.int32, (c, 1), 0)

        def body(s, carry):
            base = pl.multiple_of((s // c) * c, c)
            blk = pl.ds(base, c)
            p = (q_s[blk, :] * jnp.exp2(jnp.minimum(b_s[blk, :] - b_s[pl.ds(s, 1), :], 0.0))
                 * k_s[pl.ds(s, 1), :])
            vs = v_s[pl.ds(s, 1), :]
            seen = rowc + base >= s
            for sl in heads:
                w = jnp.where(seen, jnp.sum(p[:, sl], axis=1, keepdims=True), 0.0)
                oa_s[blk, sl] += w * vs[:, sl]
            return carry

        lax.fori_loop(0, t, body, 0)

    gate = g_ref[0].astype(F32)
    o = oi_s[...] + oa_s[...]
    for sl in heads:
        y_ref[0, :, sl] = (_rms_rows(o[:, sl], gg_ref[:, sl]) * gate[:, sl]).astype(BF16)

    @pl.when(step == pl.num_programs(1) - 1)
    def _():
        sfin_ref[...] = st_s[...]


def _hgrn_scan(q, lf, k, i, g, gg, s0):
    b, l, d = q.shape
    t = _row_tile(l, SCAN_TILE)
    c = _row_tile(t, HGRN_CHUNK)
    blk = lambda bi, s: (bi, s, 0)
    st_shape = (HGRN_HEADS, HGRN_DK, HGRN_DK)
    return pl.pallas_call(
        functools.partial(_hgrn_scan_kernel, t, c),
        grid=(b, l // t),
        in_specs=[
            pl.BlockSpec((1, t, d), blk),
            pl.BlockSpec((1, t, d), blk),
            pl.BlockSpec((1, t, d), blk),
            pl.BlockSpec((1, t, d), blk),
            pl.BlockSpec((1, t, d), blk),
            _resident((1, d)),
            _resident((t, t)),
            _resident(st_shape),
        ],
        out_specs=[pl.BlockSpec((1, t, d), blk), pl.BlockSpec(st_shape, lambda bi, s: (0, 0, 0))],
        out_shape=[jax.ShapeDtypeStruct((b, l, d), BF16), jax.ShapeDtypeStruct(st_shape, F32)],
        scratch_shapes=([pltpu.VMEM(st_shape, F32)] + [pltpu.VMEM((t, d), F32)] * 2
                        + [pltpu.VMEM((t, d), BF16)] * 4 + [pltpu.VMEM((t, d), F32)] * 4),
        compiler_params=_params("arbitrary", "arbitrary"),
        name="hgrn_scan",
    )(q, lf, k, i, g, gg, _block_tril(t, c), s0)


def kernel(x, meta_tokens, attn_norm, ffn_norm, final_norm, fox_w_in, fox_b_f, fox_q_norm, fox_k_norm,
           fox_w_out, hgrn_w_in, hgrn_lower_bounds, hgrn_g_norm, hgrn_w_out, ffn_w_in, ffn_w_out):
    bsz, seq, d = x.shape
    row = lambda v: v.reshape(1, -1).astype(F32)

    order = jnp.argsort(fox_b_f[0])
    cols = (order[:, None] * FOX_HEAD_DIM + jnp.arange(FOX_HEAD_DIM)[None, :]).reshape(-1)
    fox_b_sorted = fox_b_f[0][order]

    w_fox = fox_w_in[0, :, :4 * d].reshape(d, 4, d)[:, :, cols].reshape(d, 4 * d).astype(BF16)
    w_f = jnp.pad(fox_w_in[0, :, 4 * d:][:, order], ((0, 0), (0, LANES - FOX_HEADS))).astype(BF16)
    w_fox_out = fox_w_out[0][cols, :].astype(BF16)
    w_hgrn = hgrn_w_in[0].astype(BF16)
    w_hgrn_out = hgrn_w_out[0].astype(BF16)
    w_ffn_in = ffn_w_in.astype(BF16)
    w_ffn_out = ffn_w_out.astype(BF16)
    head_of = np.arange(d) // FOX_HEAD_DIM
    bd = jnp.asarray(head_of[:, None] == np.arange(LANES)[None, :], BF16)
    ex = jnp.concatenate([bd.T, bd.T], axis=0)
    qg = row(jnp.tile(fox_q_norm[0], FOX_HEADS)) * (FOX_HEAD_DIM ** -0.5 * LOG2E)
    kg = row(jnp.tile(fox_k_norm[0], FOX_HEADS))
    bound2 = (1.01 * LOG2E * FOX_HEAD_DIM ** 0.5 * jnp.max(jnp.abs(fox_q_norm[0]))
              * jnp.max(jnp.abs(fox_k_norm[0]))).astype(BF16).astype(F32)
    flag = (2.0 * bound2 <= GUARD_BITS).astype(jnp.int32).reshape(1)
    shifted = (bound2 + jnp.maximum(GUARD_BITS - 2.0 * bound2, 0.0)).astype(BF16).astype(F32)
    pq, pk = _placement(shifted)
    b_f = jnp.pad(row(fox_b_sorted), ((0, 0), (0, LANES - FOX_HEADS)))
    gg = row(jnp.tile(hgrn_g_norm[0], HGRN_HEADS))
    lbp = hgrn_lower_bounds.astype(F32)

    h_meta = meta_tokens.astype(F32)
    h_real = x.reshape(bsz * seq, d)
    seq3 = lambda a: a.reshape(bsz, seq, -1)

    proj = lambda h, n, c0: _fox_proj(h, n, row(attn_norm[0]), w_fox, w_f, b_f, c0, bd, ex, qg, kg, pq, pk)
    qm, km, vm, gm, c_meta = proj(h_meta, N_META, jnp.zeros((1, LANES), F32))
    qr, kr, vr, gr, c_tiles = proj(h_real, seq, c_meta[0])
    ym = _fox_attn(flag, jnp.zeros((FOX_PAIRS,), jnp.int32), qm[:, None], km[:, None], vm[None], gm[None])[0]
    pad_bias = np.zeros((FOX_HEADS, LANES, LANES), np.float32)
    for h in range(FOX_HEADS):
        pad_bias[h, N_META:, _bound_lane(h)] = PAD_KEY_BIAS
    km_pad = jnp.pad(km, ((0, 0), (0, LANES - N_META), (0, 0))) + jnp.asarray(pad_bias, BF16)
    vm_pad = jnp.pad(vm, ((0, LANES - N_META), (0, 0)))
    heads4 = lambda a: a.reshape(FOX_HEADS, bsz, seq, LANES)
    yr = _fox_attn(flag, _first_key_tile(c_tiles, c_meta[0], bsz, shifted - bound2), heads4(qr), heads4(kr), seq3(vr), seq3(gr),
                   meta=(km_pad, vm_pad)).reshape(bsz * seq, d)
    ffn0 = lambda y, h: _ffn(y, w_fox_out, h, row(ffn_norm[0]), w_ffn_in[0], w_ffn_out[0],
                             row(final_norm), False)
    h_meta = ffn0(ym, h_meta)
    h_real = ffn0(yr, h_real)

    proj = lambda h: _hgrn_proj(h, row(attn_norm[1]), w_hgrn, lbp)
    s0 = jnp.zeros((HGRN_HEADS, HGRN_DK, HGRN_DK), F32)
    _, s_meta = _hgrn_scan(*(a[None] for a in proj(h_meta)), gg, s0)
    yr, _ = _hgrn_scan(*(seq3(a) for a in proj(h_real)), gg, s_meta)
    out = _ffn(yr.reshape(bsz * seq, d), w_hgrn_out, h_real, row(ffn_norm[1]), w_ffn_in[1], w_ffn_out[1],
               row(final_norm), True)
    return out.reshape(bsz, seq, d)
```

```python
import functools
import math

import numpy as np
import jax
import jax.numpy as jnp
from jax import lax
from jax.experimental import pallas as pl
from jax.experimental.pallas import tpu as pltpu

D_MODEL = 1024
N_META = 16
FOX_HEADS = 16
FOX_HEAD_DIM = 64
HGRN_HEADS = 8
HGRN_DK = 128
HGRN_CHUNK = 64
FFN_HIDDEN = 2816
EPS = 1e-6

LANES = 128
FOX_PAIRS = FOX_HEADS // 2
FFN_TILE = 256
NEG_BIG = -1e30
PAD_KEY_BIAS = -30000.0
LOG2E = math.log2(math.e)
GUARD_BITS = 73.0
SKIP_BITS = 152.0
FAST_RANGE = 56.0
VMEM_LIMIT = 52 * 1024 * 1024

ROW_TILE = 512
ATTN_TILE = 512
ATTN_SUBTILES = 4
SCAN_TILE = 256

AUG_C = 0
AUG_ONE = 3
AUG_BOUND = 6
C3_MID, C3_LO, C3_ONE = 16, 32, 48

F32 = jnp.float32
BF16 = jnp.bfloat16

NN = (((1,), (0,)), ((), ()))
NT = (((1,), (1,)), ((), ()))
TN = (((0,), (0,)), ((), ()))


def _dot(a, b, dims=NN):
    return lax.dot_general(a, b, dims, preferred_element_type=F32)


def _params(*sem):
    return pltpu.CompilerParams(dimension_semantics=sem, vmem_limit_bytes=VMEM_LIMIT)


def _row_tile(m, pref):
    return pref if m % pref == 0 else m


def _resident(shape):
    return pl.BlockSpec(shape, lambda *_: (0,) * len(shape), pipeline_mode=pl.Buffered(1))


def _sigmoid(x):
    return 1.0 / (1.0 + jnp.exp(-x))


def _rms_rows(x, gain):
    ms = jnp.mean(x * x, axis=-1, keepdims=True)
    return x * lax.rsqrt(ms + EPS) * gain


def _split3(x):
    hi = x.astype(BF16)
    r1 = x - hi.astype(F32)
    mid = r1.astype(BF16)
    return hi, mid, (r1 - mid.astype(F32)).astype(BF16)


def _tri_cumsum(tri, x, pieces):
    if pieces == 3:
        hi, mid, lo = _split3(x)
        return _dot(tri, hi) + _dot(tri, mid) + _dot(tri, lo)
    hi = x.astype(BF16)
    return _dot(tri, hi) + _dot(tri, (x - hi.astype(F32)).astype(BF16))


def _block_tril(t, c):
    r = np.arange(t)
    return jnp.asarray((r[None, :] <= r[:, None]) & (r[None, :] // c == r[:, None] // c), BF16)


def _fox_proj_kernel(tiles_per_seq, x_ref, gain_ref, w_ref, wf_ref, bias_ref, c0_ref, tri_ref, bd_ref, ex_ref,
                     qg_ref, kg_ref, pq_ref, pk_ref, q_out, k_out, v_out, g_out, c_end, carry_s):
    @pl.when(lax.rem(pl.program_id(0), tiles_per_seq) == 0)
    def _():
        carry_s[...] = c0_ref[...]

    xn = _rms_rows(x_ref[...], gain_ref[...]).astype(BF16)
    lane = lax.broadcasted_iota(jnp.int32, (1, LANES), 1)
    lo = lane < FOX_HEAD_DIM

    zz = _dot(xn, wf_ref[...]) + bias_ref[...]
    log_f = jnp.minimum(zz, 0.0) - jnp.log1p(jnp.exp(-jnp.abs(zz)))
    c = _tri_cumsum(tri_ref[...], log_f, 3) + carry_s[...]
    tm = c.shape[0]
    carry_s[...] = c[tm - 1:tm, :]
    c_end[0] = c[tm - 1:tm, :]

    cc = jnp.where(lane < FOX_HEADS, c * LOG2E, 0.0)
    hi, mid, low = _split3(cc)
    c3 = (hi.astype(F32) + pltpu.roll(mid.astype(F32), C3_MID, axis=1)
          + pltpu.roll(low.astype(F32), C3_LO, axis=1) + jnp.where(lane == C3_ONE, 1.0, 0.0)).astype(BF16)

    def head_slabs(acc, gain, place_ref, out):
        ss = _dot((acc * acc).astype(BF16), bd_ref[...])
        r = lax.rsqrt(ss * (1.0 / FOX_HEAD_DIM) + EPS)
        r_hi = r.astype(BF16)
        r_lo = (r - r_hi.astype(F32)).astype(BF16)
        xh = acc * _dot(jnp.concatenate([r_hi, r_lo], axis=1), ex_ref[...]) * gain
        aug = _dot(c3, place_ref[...])
        for p in range(FOX_PAIRS):
            pair = slice(p * LANES, (p + 1) * LANES)
            out[2 * p] = jnp.where(lo, xh[:, pair], aug[:, pair]).astype(BF16)
            out[2 * p + 1] = jnp.where(lo, aug[:, pair], xh[:, pair]).astype(BF16)

    d = D_MODEL
    head_slabs(_dot(xn, w_ref[:, 0:d]), qg_ref[...], pq_ref, q_out)
    head_slabs(_dot(xn, w_ref[:, d:2 * d]), kg_ref[...], pk_ref, k_out)
    v_out[...] = _dot(xn, w_ref[:, 2 * d:3 * d]).astype(BF16)
    g_out[...] = _dot(xn, w_ref[:, 3 * d:4 * d])


def _fox_proj(x, seq, gain, w, wf, bias, c0, bd, ex, qg, kg, pq, pk):
    m = x.shape[0]
    tm = _row_tile(seq, ROW_TILE)
    d = D_MODEL
    row = lambda i: (i, 0)
    slab = pl.BlockSpec((FOX_HEADS, tm, LANES), lambda i: (0, i, 0))
    return pl.pallas_call(
        functools.partial(_fox_proj_kernel, seq // tm),
        grid=(m // tm,),
        in_specs=[
            pl.BlockSpec((tm, d), row),
            _resident((1, d)),
            _resident((d, 4 * d)),
            _resident((d, LANES)),
            _resident((1, LANES)),
            _resident((1, LANES)),
            _resident((tm, tm)),
            _resident((d, LANES)),
            _resident((2 * LANES, d)),
            _resident((1, d)),
            _resident((1, d)),
            _resident((LANES, d)),
            _resident((LANES, d)),
        ],
        out_specs=[slab, slab, pl.BlockSpec((tm, d), row), pl.BlockSpec((tm, d), row),
                   pl.BlockSpec((1, 1, LANES), lambda i: (i, 0, 0))],
        out_shape=[
            jax.ShapeDtypeStruct((FOX_HEADS, m, LANES), BF16),
            jax.ShapeDtypeStruct((FOX_HEADS, m, LANES), BF16),
            jax.ShapeDtypeStruct((m, d), BF16),
            jax.ShapeDtypeStruct((m, d), F32),
            jax.ShapeDtypeStruct((m // tm, 1, LANES), F32),
        ],
        scratch_shapes=[pltpu.VMEM((1, LANES), F32)],
        compiler_params=_params("arbitrary"),
        name="fox_proj",
    )(x, gain, w, wf, bias, c0, _block_tril(tm, tm), bd, ex, qg, kg, pq, pk)


def _bound_lane(h):
    return (FOX_HEAD_DIM if h % 2 == 0 else 0) + AUG_BOUND


def _placement(bound2):
    pq = np.zeros((LANES, D_MODEL), np.float32)
    pk = np.zeros((LANES, D_MODEL), np.float32)
    pb = np.zeros((LANES, D_MODEL), np.float32)
    for h in range(FOX_HEADS):
        base = (h // 2) * LANES + (FOX_HEAD_DIM if h % 2 == 0 else 0)
        for piece, src in enumerate((h, C3_MID + h, C3_LO + h)):
            pq[src, base + AUG_C + piece] = 1.0
            pk[src, base + AUG_ONE + piece] = -1.0
            pq[C3_ONE, base + AUG_ONE + piece] = 1.0
            pk[C3_ONE, base + AUG_C + piece] = 1.0
        pq[C3_ONE, base + AUG_BOUND] = 1.0
        pb[C3_ONE, base + AUG_BOUND] = 1.0
    return jnp.asarray(pq, BF16), (jnp.asarray(pk) - jnp.asarray(pb) * bound2).astype(BF16)


def _first_key_tile(c_tile_end, c_start, bsz, shift):
    nt = c_tile_end.shape[0] // bsz
    ce = c_tile_end.reshape(bsz, nt, LANES)[:, :, :FOX_HEADS] * LOG2E
    before = jnp.concatenate([jnp.broadcast_to(c_start[:, :FOX_HEADS] * LOG2E, (bsz, 1, FOX_HEADS)),
                              ce[:, :-1]], axis=1)
    gap = before[:, :, None, :] - ce[:, None, :, :]
    earlier = (jnp.arange(nt)[None, :] < jnp.arange(nt)[:, None])[None, :, :, None]
    dead = jnp.logical_and(gap - shift < -SKIP_BITS, earlier)
    dead = jnp.logical_and(dead[..., 0::2], dead[..., 1::2])
    first = jnp.sum(dead.astype(jnp.int32), axis=2)
    return first.transpose(0, 2, 1).reshape(-1)


def _fox_attn_kernel(has_meta, t, nt, lp, subs, flag_ref, first_ref, q_ref, k_ref, v_ref, gate_ref, *rest):
    if has_meta:
        km_ref, vm_ref, o_ref, m_s, l_s, lp_s, acc_s = rest
    else:
        o_ref, m_s, l_s, lp_s, acc_s = rest

    lane = lax.broadcasted_iota(jnp.int32, (1, LANES), 1)
    lo = lane < FOX_HEAD_DIM
    half = (lo, jnp.logical_not(lo))

    def run(online, sub):
        i = pl.program_id(2) * subs + sub
        rows = pl.ds(pl.multiple_of(sub * t, t), t)
        first = first_ref[(pl.program_id(0) * FOX_PAIRS + pl.program_id(1)) * nt + i]

        def tile(keys, vt, mask):
            alphas, pv = [], None
            zero_v = jnp.zeros_like(vt)
            for a in range(2):
                s = _dot(q_ref[a, 0, rows, :], keys(a), NT)
                if mask is not None:
                    s = jnp.where(mask, s, NEG_BIG)
                if online:
                    m_prev = m_s[a]
                    m_new = jnp.maximum(m_prev, jnp.max(s, axis=1, keepdims=True))
                    alpha = jnp.exp2(m_prev - m_new)
                    p = jnp.exp2(s - m_new)
                    l_s[a] = alpha * l_s[a] + jnp.sum(p, axis=1, keepdims=True)
                    m_s[a] = m_new
                    alphas.append(alpha)
                else:
                    p = jnp.exp2(s)
                    part = lp_s[a]
                    for c0 in range(0, p.shape[1], lp):
                        part = part + p[:, c0:c0 + lp]
                    lp_s[a] = part
                d = _dot(p.astype(BF16), jnp.where(half[a], vt, zero_v))
                pv = d if pv is None else pv + d
            if online:
                acc_s[...] = acc_s[...] * jnp.where(lo, alphas[0], alphas[1]) + pv
            else:
                acc_s[...] += pv

        def real_tile(off, size, mask):
            tile(lambda a: k_ref[a, 0, pl.ds(off, size), :], v_ref[0, pl.ds(off, size), :], mask)

        if online:
            m_s[...] = jnp.full(m_s.shape, NEG_BIG, F32)
            l_s[...] = jnp.zeros(l_s.shape, F32)
        else:
            lp_s[...] = jnp.zeros(lp_s.shape, F32)
        acc_s[...] = jnp.zeros(acc_s.shape, F32)

        if nt >= 2:
            lead = jnp.int32(0) if online else first
            n = i - lead

            @pl.when((n & 1) == 1)
            def _():
                real_tile(pl.multiple_of(lead * t, t), t, None)

            def body(j, carry):
                real_tile(pl.multiple_of((lead + (n & 1) + 2 * j) * t, t), 2 * t, None)
                return carry

            lax.fori_loop(0, lax.shift_right_logical(n, 1), body, 0)

        off = pl.multiple_of(i * t, t)

        def diag(n_pre):
            row = lax.broadcasted_iota(jnp.int32, (t, n_pre + t), 0)
            col = lax.broadcasted_iota(jnp.int32, (t, n_pre + t), 1)
            if n_pre:
                tile(lambda a: jnp.concatenate([km_ref[a], k_ref[a, 0, pl.ds(off, t), :]], axis=0),
                     jnp.concatenate([vm_ref[...], v_ref[0, pl.ds(off, t), :]], axis=0),
                     col - n_pre <= row)
            else:
                real_tile(off, t, col <= row)

        if has_meta and online:
            diag(LANES)
        elif has_meta:
            pl.when(first == 0)(lambda: diag(LANES))
            pl.when(first > 0)(lambda: diag(0))
        else:
            diag(0)

        if online:
            inv = jnp.where(lo, 1.0 / l_s[0], 1.0 / l_s[1])
        else:
            inv = jnp.where(lo, 1.0 / jnp.sum(lp_s[0], axis=1, keepdims=True),
                            1.0 / jnp.sum(lp_s[1], axis=1, keepdims=True))
        o_ref[0, rows, :] = (acc_s[...] * inv * _sigmoid(gate_ref[0, rows, :])).astype(BF16)

    def steps(online):
        if subs == 1:
            run(online, 0)
        else:
            def body(sub, carry):
                run(online, sub)
                return carry

            lax.fori_loop(0, subs, body, 0)

    @pl.when(flag_ref[0] == 1)
    def _():
        steps(False)

    @pl.when(flag_ref[0] != 1)
    def _():
        steps(True)


def _fox_attn(flag, first, q, k, v, gate, meta=None):
    b, l, d = v.shape
    t = _row_tile(l, ATTN_TILE)
    nt = l // t
    subs = ATTN_SUBTILES if nt % ATTN_SUBTILES == 0 else 1
    lp = LANES if t % LANES == 0 else t
    in_specs = [
        pl.BlockSpec(memory_space=pltpu.SMEM),
        pl.BlockSpec(memory_space=pltpu.SMEM),
        pl.BlockSpec((2, 1, subs * t, LANES), lambda bi, hp, i: (hp, bi, i, 0)),
        pl.BlockSpec((2, 1, l, LANES), lambda bi, hp, i: (hp, bi, 0, 0)),
        pl.BlockSpec((1, l, LANES), lambda bi, hp, i: (bi, 0, hp)),
        pl.BlockSpec((1, subs * t, LANES), lambda bi, hp, i: (bi, i, hp)),
    ]
    args = [flag, first, q, k, v, gate]
    if meta is not None:
        km, vm = meta
        in_specs += [
            pl.BlockSpec((2, LANES, LANES), lambda bi, hp, i: (hp, 0, 0)),
            pl.BlockSpec((LANES, LANES), lambda bi, hp, i: (0, hp)),
        ]
        args += [km, vm]
    return pl.pallas_call(
        functools.partial(_fox_attn_kernel, meta is not None, t, nt, lp, subs),
        grid=(b, FOX_PAIRS, nt // subs),
        in_specs=in_specs,
        out_specs=pl.BlockSpec((1, subs * t, LANES), lambda bi, hp, i: (bi, i, hp)),
        out_shape=jax.ShapeDtypeStruct((b, l, d), BF16),
        scratch_shapes=[
            pltpu.VMEM((2, t, 1), F32),
            pltpu.VMEM((2, t, 1), F32),
            pltpu.VMEM((2, t, lp), F32),
            pltpu.VMEM((t, LANES), F32),
        ],
        compiler_params=_params("arbitrary", "arbitrary", "arbitrary"),
        name="fox_attn",
    )(*args)


def _ffn_kernel(final, nh, y_ref, wmix_ref, h_ref, gain_ref, win_ref, wout_ref, fgain_ref, o_ref,
                xn_s, acc_s):
    hm = h_ref[...] + _dot(y_ref[...], wmix_ref[...])
    xn_s[...] = _rms_rows(hm, gain_ref[...]).astype(BF16)
    acc_s[...] = hm

    for j in range(nh):
        xn = xn_s[...]
        g = _dot(xn, win_ref[:, j * FFN_TILE:(j + 1) * FFN_TILE])
        u = _dot(xn, win_ref[:, FFN_HIDDEN + j * FFN_TILE:FFN_HIDDEN + (j + 1) * FFN_TILE])
        acc_s[...] += _dot((g * _sigmoid(g) * u).astype(BF16), wout_ref[j * FFN_TILE:(j + 1) * FFN_TILE, :])

    out = acc_s[...]
    if final:
        out = _rms_rows(out, fgain_ref[...])
    o_ref[...] = out


def _ffn(y, w_mix, h, gain, w_in, w_out, fgain, final):
    m, d = h.shape
    tm = _row_tile(m, ROW_TILE)
    nh = FFN_HIDDEN // FFN_TILE
    row = lambda i: (i, 0)
    return pl.pallas_call(
        functools.partial(_ffn_kernel, final, nh),
        grid=(m // tm,),
        in_specs=[
            pl.BlockSpec((tm, d), row),
            _resident((d, d)),
            pl.BlockSpec((tm, d), row),
            _resident((1, d)),
            _resident((d, 2 * FFN_HIDDEN)),
            _resident((FFN_HIDDEN, d)),
            _resident((1, d)),
        ],
        out_specs=pl.BlockSpec((tm, d), row),
        out_shape=jax.ShapeDtypeStruct((m, d), F32),
        scratch_shapes=[pltpu.VMEM((tm, d), BF16), pltpu.VMEM((tm, d), F32)],
        compiler_params=_params("arbitrary"),
        name="ffn",
    )(y, w_mix, h, gain, w_in, w_out, fgain)


def _hgrn_proj_kernel(x_ref, gain_ref, w_ref, lbp_ref, q_out, lf_out, k_out, v_out, g_out):
    d = D_MODEL

    lbp = lbp_ref[...]
    mx = jnp.maximum(lbp[0:1], lbp[1:2])
    e0 = jnp.exp(lbp[0:1] - mx)
    e1 = jnp.exp(lbp[1:2] - mx)
    den = e0 + e1
    sm0 = e0 / den
    lb = (sm0 + e1 / den) - sm0

    tm = x_ref.shape[0]
    hr = tm // 2 if tm % 32 == 0 else tm
    for r0 in range(0, tm, hr):
        rs = slice(r0, r0 + hr)
        xh = _rms_rows(x_ref[rs, :], gain_ref[...]).astype(BF16)
        qz = _dot(xh, w_ref[:, 0:d])
        q_out[rs, :] = (qz * _sigmoid(qz)).astype(BF16)
        sig = _sigmoid(_dot(xh, w_ref[:, d:2 * d]))
        lf_out[rs, :] = jnp.log2(lb + (1.0 - lb) * sig)
        k_out[rs, :] = (1.0 - lb) * (1.0 - sig)
        v_out[rs, :] = _dot(xh, w_ref[:, 2 * d:3 * d]).astype(BF16)
        gz = _dot(xh, w_ref[:, 3 * d:4 * d])
        g_out[rs, :] = (gz * _sigmoid(gz)).astype(BF16)


def _hgrn_proj(x, gain, w, lbp):
    m, d = x.shape
    tm = _row_tile(m, ROW_TILE)
    row = lambda i: (i, 0)
    return pl.pallas_call(
        _hgrn_proj_kernel,
        grid=(m // tm,),
        in_specs=[pl.BlockSpec((tm, d), row), _resident((1, d)), _resident((d, 4 * d)), _resident((2, d))],
        out_specs=[pl.BlockSpec((tm, d), row)] * 5,
        out_shape=[jax.ShapeDtypeStruct((m, d), BF16), jax.ShapeDtypeStruct((m, d), F32),
                   jax.ShapeDtypeStruct((m, d), F32), jax.ShapeDtypeStruct((m, d), BF16),
                   jax.ShapeDtypeStruct((m, d), BF16)],
        compiler_params=_params("arbitrary"),
        name="hgrn_proj",
    )(x, gain, w, lbp)


def _hgrn_scan_kernel(t, c, q_ref, lf_ref, k_ref, i_ref, g_ref, gg_ref, tri_ref, s0_ref,
                      y_ref, sfin_ref, st_s, oi_s, oa_s, qin_s, kout_s, qf_s, kf_s, b_s, q_s, k_s, v_s):
    step = pl.program_id(1)
    nc = t // c
    mid = c // 2
    heads = [slice(h * HGRN_DK, (h + 1) * HGRN_DK) for h in range(HGRN_HEADS)]

    @pl.when(step == 0)
    def _():
        st_s[...] = s0_ref[...]

    qq = q_ref[0].astype(F32)
    kk = k_ref[0]
    v_b = i_ref[0]

    b = _tri_cumsum(tri_ref[...], lf_ref[0], 2)

    spread = jnp.float32(0.0)
    decay = []
    for ci in range(nc):
        rs = slice(ci * c, (ci + 1) * c)
        bc = b[rs]
        b_last = bc[c - 1:c]
        b_mid = bc[mid - 1:mid]
        spread = jnp.maximum(spread, jnp.maximum(jnp.max(bc[0:1] - b_mid), jnp.max(b_mid - b_last)))
        q_in = qq[rs] * jnp.exp2(bc)
        k_out = kk[rs] * jnp.exp2(b_last - bc)
        qin_s[rs] = q_in.astype(BF16)
        kout_s[rs] = k_out.astype(BF16)
        qf_s[rs] = (q_in * jnp.exp2(-b_mid)).astype(BF16)
        kf_s[rs] = (k_out * jnp.exp2(b_mid - b_last)).astype(BF16)
        decay.append(jnp.exp2(b_last))

    keep = tri_ref[...] > 0
    for sl in heads:
        a = jnp.where(keep, _dot(qf_s[:, sl], kf_s[:, sl], NT), 0.0)
        oa_s[:, sl] = _dot(a.astype(BF16), v_b[:, sl])

    for ci in range(nc):
        rs = slice(ci * c, (ci + 1) * c)
        for h, sl in enumerate(heads):
            st = st_s[h]
            oi_s[rs, sl] = _dot(qin_s[rs, sl], st.astype(BF16), NT)
            st_s[h] = st * decay[ci][:, sl] + _dot(v_b[rs, sl], kout_s[rs, sl], TN)

    @pl.when(spread > FAST_RANGE)
    def _():
        b_s[...] = b
        k_s[...] = kk
        q_s[...] = qq
        v_s[...] = v_b.astype(F32)
        oa_s[...] = jnp.zeros(oa_s.shape, F32)
        rowc = lax.broadcasted_iota(jnp.int32, (c, 1), 0)

        def body(s, carry):
            base = pl.multiple_of((s // c) * c, c)
            blk = pl.ds(base, c)
            p = (q_s[blk, :] * jnp.exp2(jnp.minimum(b_s[blk, :] - b_s[pl.ds(s, 1), :], 0.0))
                 * k_s[pl.ds(s, 1), :])
            vs = v_s[pl.ds(s, 1), :]
            seen = rowc + base >= s
            for sl in heads:
                w = jnp.where(seen, jnp.sum(p[:, sl], axis=1, keepdims=True), 0.0)
                oa_s[blk, sl] += w * vs[:, sl]
            return carry

        lax.fori_loop(0, t, body, 0)

    gate = g_ref[0].astype(F32)
    o = oi_s[...] + oa_s[...]
    for sl in heads:
        y_ref[0, :, sl] = (_rms_rows(o[:, sl], gg_ref[:, sl]) * gate[:, sl]).astype(BF16)

    @pl.when(step == pl.num_programs(1) - 1)
    def _():
        sfin_ref[...] = st_s[...]


def _hgrn_scan(q, lf, k, i, g, gg, s0):
    b, l, d = q.shape
    t = _row_tile(l, SCAN_TILE)
    c = _row_tile(t, HGRN_CHUNK)
    blk = lambda bi, s: (bi, s, 0)
    st_shape = (HGRN_HEADS, HGRN_DK, HGRN_DK)
    return pl.pallas_call(
        functools.partial(_hgrn_scan_kernel, t, c),
        grid=(b, l // t),
        in_specs=[
            pl.BlockSpec((1, t, d), blk),
            pl.BlockSpec((1, t, d), blk),
            pl.BlockSpec((1, t, d), blk),
            pl.BlockSpec((1, t, d), blk),
            pl.BlockSpec((1, t, d), blk),
            _resident((1, d)),
            _resident((t, t)),
            _resident(st_shape),
        ],
        out_specs=[pl.BlockSpec((1, t, d), blk), pl.BlockSpec(st_shape, lambda bi, s: (0, 0, 0))],
        out_shape=[jax.ShapeDtypeStruct((b, l, d), BF16), jax.ShapeDtypeStruct(st_shape, F32)],
        scratch_shapes=([pltpu.VMEM(st_shape, F32)] + [pltpu.VMEM((t, d), F32)] * 2
                        + [pltpu.VMEM((t, d), BF16)] * 4 + [pltpu.VMEM((t, d), F32)] * 4),
        compiler_params=_params("arbitrary", "arbitrary"),
        name="hgrn_scan",
    )(q, lf, k, i, g, gg, _block_tril(t, c), s0)


def kernel(x, meta_tokens, attn_norm, ffn_norm, final_norm, fox_w_in, fox_b_f, fox_q_norm, fox_k_norm,
           fox_w_out, hgrn_w_in, hgrn_lower_bounds, hgrn_g_norm, hgrn_w_out, ffn_w_in, ffn_w_out):
    bsz, seq, d = x.shape
    row = lambda v: v.reshape(1, -1).astype(F32)

    order = jnp.argsort(fox_b_f[0])
    cols = (order[:, None] * FOX_HEAD_DIM + jnp.arange(FOX_HEAD_DIM)[None, :]).reshape(-1)
    fox_b_sorted = fox_b_f[0][order]

    w_fox = fox_w_in[0, :, :4 * d].reshape(d, 4, d)[:, :, cols].reshape(d, 4 * d).astype(BF16)
    w_f = jnp.pad(fox_w_in[0, :, 4 * d:][:, order], ((0, 0), (0, LANES - FOX_HEADS))).astype(BF16)
    w_fox_out = fox_w_out[0][cols, :].astype(BF16)
    w_hgrn = hgrn_w_in[0].astype(BF16)
    w_hgrn_out = hgrn_w_out[0].astype(BF16)
    w_ffn_in = ffn_w_in.astype(BF16)
    w_ffn_out = ffn_w_out.astype(BF16)
    head_of = np.arange(d) // FOX_HEAD_DIM
    bd = jnp.asarray(head_of[:, None] == np.arange(LANES)[None, :], BF16)
    ex = jnp.concatenate([bd.T, bd.T], axis=0)
    qg = row(jnp.tile(fox_q_norm[0], FOX_HEADS)) * (FOX_HEAD_DIM ** -0.5 * LOG2E)
    kg = row(jnp.tile(fox_k_norm[0], FOX_HEADS))
    bound2 = (1.01 * LOG2E * FOX_HEAD_DIM ** 0.5 * jnp.max(jnp.abs(fox_q_norm[0]))
              * jnp.max(jnp.abs(fox_k_norm[0]))).astype(BF16).astype(F32)
    flag = (2.0 * bound2 <= GUARD_BITS).astype(jnp.int32).reshape(1)
    shifted = (bound2 + jnp.maximum(GUARD_BITS - 2.0 * bound2, 0.0)).astype(BF16).astype(F32)
    pq, pk = _placement(shifted)
    b_f = jnp.pad(row(fox_b_sorted), ((0, 0), (0, LANES - FOX_HEADS)))
    gg = row(jnp.tile(hgrn_g_norm[0], HGRN_HEADS))
    lbp = hgrn_lower_bounds.astype(F32)

    h_meta = meta_tokens.astype(F32)
    h_real = x.reshape(bsz * seq, d)
    seq3 = lambda a: a.reshape(bsz, seq, -1)

    proj = lambda h, n, c0: _fox_proj(h, n, row(attn_norm[0]), w_fox, w_f, b_f, c0, bd, ex, qg, kg, pq, pk)
    qm, km, vm, gm, c_meta = proj(h_meta, N_META, jnp.zeros((1, LANES), F32))
    qr, kr, vr, gr, c_tiles = proj(h_real, seq, c_meta[0])
    ym = _fox_attn(flag, jnp.zeros((FOX_PAIRS,), jnp.int32), qm[:, None], km[:, None], vm[None], gm[None])[0]
    pad_bias = np.zeros((FOX_HEADS, LANES, LANES), np.float32)
    for h in range(FOX_HEADS):
        pad_bias[h, N_META:, _bound_lane(h)] = PAD_KEY_BIAS
    km_pad = jnp.pad(km, ((0, 0), (0, LANES - N_META), (0, 0))) + jnp.asarray(pad_bias, BF16)
    vm_pad = jnp.pad(vm, ((0, LANES - N_META), (0, 0)))
    heads4 = lambda a: a.reshape(FOX_HEADS, bsz, seq, LANES)
    yr = _fox_attn(flag, _first_key_tile(c_tiles, c_meta[0], bsz, shifted - bound2), heads4(qr), heads4(kr), seq3(vr), seq3(gr),
                   meta=(km_pad, vm_pad)).reshape(bsz * seq, d)
    ffn0 = lambda y, h: _ffn(y, w_fox_out, h, row(ffn_norm[0]), w_ffn_in[0], w_ffn_out[0],
                             row(final_norm), False)
    h_meta = ffn0(ym, h_meta)
    h_real = ffn0(yr, h_real)

    proj = lambda h: _hgrn_proj(h, row(attn_norm[1]), w_hgrn, lbp)
    s0 = jnp.zeros((HGRN_HEADS, HGRN_DK, HGRN_DK), F32)
    _, s_meta = _hgrn_scan(*(a[None] for a in proj(h_meta)), gg, s0)
    yr, _ = _hgrn_scan(*(seq3(a) for a in proj(h_real)), gg, s_meta)
    out = _ffn(yr.reshape(bsz * seq, d), w_hgrn_out, h_real, row(ffn_norm[1]), w_ffn_in[1], w_ffn_out[1],
               row(final_norm), True)
    return out.reshape(bsz, seq, d)
```

```python
import functools
import math

import numpy as np
import jax
import jax.numpy as jnp
from jax import lax
from jax.experimental import pallas as pl
from jax.experimental.pallas import tpu as pltpu

D_MODEL = 1024
N_META = 16
FOX_HEADS = 16
FOX_HEAD_DIM = 64
HGRN_HEADS = 8
HGRN_DK = 128
HGRN_CHUNK = 64
FFN_HIDDEN = 2816
EPS = 1e-6

LANES = 128
FOX_PAIRS = FOX_HEADS // 2
FFN_TILE = 256
NEG_BIG = -1e30
PAD_KEY_BIAS = -30000.0
LOG2E = math.log2(math.e)
GUARD_BITS = 73.0
SKIP_BITS = 152.0
FAST_RANGE = 56.0
VMEM_LIMIT = 52 * 1024 * 1024

ROW_TILE = 512
ATTN_TILE = 512
ATTN_SUBTILES = 16
SCAN_TILE = 256

AUG_C = 0
AUG_ONE = 3
AUG_BOUND = 6
C3_MID, C3_LO, C3_ONE = 16, 32, 48

F32 = jnp.float32
BF16 = jnp.bfloat16

NN = (((1,), (0,)), ((), ()))
NT = (((1,), (1,)), ((), ()))
TN = (((0,), (0,)), ((), ()))


def _dot(a, b, dims=NN):
    return lax.dot_general(a, b, dims, preferred_element_type=F32)


def _params(*sem):
    return pltpu.CompilerParams(dimension_semantics=sem, vmem_limit_bytes=VMEM_LIMIT)


def _row_tile(m, pref):
    return pref if m % pref == 0 else m


def _resident(shape):
    return pl.BlockSpec(shape, lambda *_: (0,) * len(shape), pipeline_mode=pl.Buffered(1))


def _sigmoid(x):
    return 1.0 / (1.0 + jnp.exp(-x))


def _rms_rows(x, gain):
    ms = jnp.mean(x * x, axis=-1, keepdims=True)
    return x * lax.rsqrt(ms + EPS) * gain


def _split3(x):
    hi = x.astype(BF16)
    r1 = x - hi.astype(F32)
    mid = r1.astype(BF16)
    return hi, mid, (r1 - mid.astype(F32)).astype(BF16)


def _tri_cumsum(tri, x, pieces):
    if pieces == 3:
        hi, mid, lo = _split3(x)
        return _dot(tri, hi) + _dot(tri, mid) + _dot(tri, lo)
    hi = x.astype(BF16)
    return _dot(tri, hi) + _dot(tri, (x - hi.astype(F32)).astype(BF16))


def _block_tril(t, c):
    r = np.arange(t)
    return jnp.asarray((r[None, :] <= r[:, None]) & (r[None, :] // c == r[:, None] // c), BF16)


def _fox_proj_kernel(tiles_per_seq, x_ref, gain_ref, w_ref, wf_ref, bias_ref, c0_ref, tri_ref, bd_ref, ex_ref,
                     qg_ref, kg_ref, pq_ref, pk_ref, q_out, k_out, v_out, g_out, c_end, carry_s):
    @pl.when(lax.rem(pl.program_id(0), tiles_per_seq) == 0)
    def _():
        carry_s[...] = c0_ref[...]

    xn = _rms_rows(x_ref[...], gain_ref[...]).astype(BF16)
    lane = lax.broadcasted_iota(jnp.int32, (1, LANES), 1)
    lo = lane < FOX_HEAD_DIM

    zz = _dot(xn, wf_ref[...]) + bias_ref[...]
    log_f = jnp.minimum(zz, 0.0) - jnp.log1p(jnp.exp(-jnp.abs(zz)))
    c = _tri_cumsum(tri_ref[...], log_f, 3) + carry_s[...]
    tm = c.shape[0]
    carry_s[...] = c[tm - 1:tm, :]
    c_end[0] = c[tm - 1:tm, :]

    cc = jnp.where(lane < FOX_HEADS, c * LOG2E, 0.0)
    hi, mid, low = _split3(cc)
    c3 = (hi.astype(F32) + pltpu.roll(mid.astype(F32), C3_MID, axis=1)
          + pltpu.roll(low.astype(F32), C3_LO, axis=1) + jnp.where(lane == C3_ONE, 1.0, 0.0)).astype(BF16)

    def head_slabs(acc, gain, place_ref, out):
        ss = _dot((acc * acc).astype(BF16), bd_ref[...])
        r = lax.rsqrt(ss * (1.0 / FOX_HEAD_DIM) + EPS)
        r_hi = r.astype(BF16)
        r_lo = (r - r_hi.astype(F32)).astype(BF16)
        xh = acc * _dot(jnp.concatenate([r_hi, r_lo], axis=1), ex_ref[...]) * gain
        aug = _dot(c3, place_ref[...])
        for p in range(FOX_PAIRS):
            pair = slice(p * LANES, (p + 1) * LANES)
            out[2 * p] = jnp.where(lo, xh[:, pair], aug[:, pair]).astype(BF16)
            out[2 * p + 1] = jnp.where(lo, aug[:, pair], xh[:, pair]).astype(BF16)

    d = D_MODEL
    head_slabs(_dot(xn, w_ref[:, 0:d]), qg_ref[...], pq_ref, q_out)
    head_slabs(_dot(xn, w_ref[:, d:2 * d]), kg_ref[...], pk_ref, k_out)
    v_out[...] = _dot(xn, w_ref[:, 2 * d:3 * d]).astype(BF16)
    g_out[...] = _dot(xn, w_ref[:, 3 * d:4 * d])


def _fox_proj(x, seq, gain, w, wf, bias, c0, bd, ex, qg, kg, pq, pk):
    m = x.shape[0]
    tm = _row_tile(seq, ROW_TILE)
    d = D_MODEL
    row = lambda i: (i, 0)
    slab = pl.BlockSpec((FOX_HEADS, tm, LANES), lambda i: (0, i, 0))
    return pl.pallas_call(
        functools.partial(_fox_proj_kernel, seq // tm),
        grid=(m // tm,),
        in_specs=[
            pl.BlockSpec((tm, d), row),
            _resident((1, d)),
            _resident((d, 4 * d)),
            _resident((d, LANES)),
            _resident((1, LANES)),
            _resident((1, LANES)),
            _resident((tm, tm)),
            _resident((d, LANES)),
            _resident((2 * LANES, d)),
            _resident((1, d)),
            _resident((1, d)),
            _resident((LANES, d)),
            _resident((LANES, d)),
        ],
        out_specs=[slab, slab, pl.BlockSpec((tm, d), row), pl.BlockSpec((tm, d), row),
                   pl.BlockSpec((1, 1, LANES), lambda i: (i, 0, 0))],
        out_shape=[
            jax.ShapeDtypeStruct((FOX_HEADS, m, LANES), BF16),
            jax.ShapeDtypeStruct((FOX_HEADS, m, LANES), BF16),
            jax.ShapeDtypeStruct((m, d), BF16),
            jax.ShapeDtypeStruct((m, d), F32),
            jax.ShapeDtypeStruct((m // tm, 1, LANES), F32),
        ],
        scratch_shapes=[pltpu.VMEM((1, LANES), F32)],
        compiler_params=_params("arbitrary"),
        name="fox_proj",
    )(x, gain, w, wf, bias, c0, _block_tril(tm, tm), bd, ex, qg, kg, pq, pk)


def _bound_lane(h):
    return (FOX_HEAD_DIM if h % 2 == 0 else 0) + AUG_BOUND


def _placement(bound2):
    pq = np.zeros((LANES, D_MODEL), np.float32)
    pk = np.zeros((LANES, D_MODEL), np.float32)
    pb = np.zeros((LANES, D_MODEL), np.float32)
    for h in range(FOX_HEADS):
        base = (h // 2) * LANES + (FOX_HEAD_DIM if h % 2 == 0 else 0)
        for piece, src in enumerate((h, C3_MID + h, C3_LO + h)):
            pq[src, base + AUG_C + piece] = 1.0
            pk[src, base + AUG_ONE + piece] = -1.0
            pq[C3_ONE, base + AUG_ONE + piece] = 1.0
            pk[C3_ONE, base + AUG_C + piece] = 1.0
        pq[C3_ONE, base + AUG_BOUND] = 1.0
        pb[C3_ONE, base + AUG_BOUND] = 1.0
    return jnp.asarray(pq, BF16), (jnp.asarray(pk) - jnp.asarray(pb) * bound2).astype(BF16)


def _first_key_tile(c_tile_end, c_start, bsz, shift):
    nt = c_tile_end.shape[0] // bsz
    ce = c_tile_end.reshape(bsz, nt, LANES)[:, :, :FOX_HEADS] * LOG2E
    before = jnp.concatenate([jnp.broadcast_to(c_start[:, :FOX_HEADS] * LOG2E, (bsz, 1, FOX_HEADS)),
                              ce[:, :-1]], axis=1)
    gap = before[:, :, None, :] - ce[:, None, :, :]
    earlier = (jnp.arange(nt)[None, :] < jnp.arange(nt)[:, None])[None, :, :, None]
    dead = jnp.logical_and(gap - shift < -SKIP_BITS, earlier)
    dead = jnp.logical_and(dead[..., 0::2], dead[..., 1::2])
    first = jnp.sum(dead.astype(jnp.int32), axis=2)
    return first.transpose(0, 2, 1).reshape(-1)


def _fox_attn_kernel(has_meta, t, nt, lp, subs, flag_ref, first_ref, q_ref, k_ref, v_ref, gate_ref, *rest):
    if has_meta:
        km_ref, vm_ref, o_ref, m_s, l_s, lp_s, acc_s = rest
    else:
        o_ref, m_s, l_s, lp_s, acc_s = rest

    lane = lax.broadcasted_iota(jnp.int32, (1, LANES), 1)
    lo = lane < FOX_HEAD_DIM
    half = (lo, jnp.logical_not(lo))

    def run(online, sub):
        i = pl.program_id(2) * subs + sub
        rows = pl.ds(pl.multiple_of(sub * t, t), t)
        first = first_ref[(pl.program_id(0) * FOX_PAIRS + pl.program_id(1)) * nt + i]

        def tile(keys, vt, mask):
            alphas, pv = [], None
            zero_v = jnp.zeros_like(vt)
            for a in range(2):
                s = _dot(q_ref[a, 0, rows, :], keys(a), NT)
                if mask is not None:
                    s = jnp.where(mask, s, NEG_BIG)
                if online:
                    m_prev = m_s[a]
                    m_new = jnp.maximum(m_prev, jnp.max(s, axis=1, keepdims=True))
                    alpha = jnp.exp2(m_prev - m_new)
                    p = jnp.exp2(s - m_new)
                    l_s[a] = alpha * l_s[a] + jnp.sum(p, axis=1, keepdims=True)
                    m_s[a] = m_new
                    alphas.append(alpha)
                else:
                    p = jnp.exp2(s)
                    part = lp_s[a]
                    for c0 in range(0, p.shape[1], lp):
                        part = part + p[:, c0:c0 + lp]
                    lp_s[a] = part
                d = _dot(p.astype(BF16), jnp.where(half[a], vt, zero_v))
                pv = d if pv is None else pv + d
            if online:
                acc_s[...] = acc_s[...] * jnp.where(lo, alphas[0], alphas[1]) + pv
            else:
                acc_s[...] += pv

        def real_tile(off, size, mask):
            tile(lambda a: k_ref[a, 0, pl.ds(off, size), :], v_ref[0, pl.ds(off, size), :], mask)

        if online:
            m_s[...] = jnp.full(m_s.shape, NEG_BIG, F32)
            l_s[...] = jnp.zeros(l_s.shape, F32)
        else:
            lp_s[...] = jnp.zeros(lp_s.shape, F32)
        acc_s[...] = jnp.zeros(acc_s.shape, F32)

        if nt >= 2:
            lead = jnp.int32(0) if online else first
            n = i - lead

            @pl.when((n & 1) == 1)
            def _():
                real_tile(pl.multiple_of(lead * t, t), t, None)

            def body(j, carry):
                real_tile(pl.multiple_of((lead + (n & 1) + 2 * j) * t, t), 2 * t, None)
                return carry

            lax.fori_loop(0, lax.shift_right_logical(n, 1), body, 0)

        off = pl.multiple_of(i * t, t)

        def diag(n_pre):
            row = lax.broadcasted_iota(jnp.int32, (t, n_pre + t), 0)
            col = lax.broadcasted_iota(jnp.int32, (t, n_pre + t), 1)
            if n_pre:
                tile(lambda a: jnp.concatenate([km_ref[a], k_ref[a, 0, pl.ds(off, t), :]], axis=0),
                     jnp.concatenate([vm_ref[...], v_ref[0, pl.ds(off, t), :]], axis=0),
                     col - n_pre <= row)
            else:
                real_tile(off, t, col <= row)

        if has_meta and online:
            diag(LANES)
        elif has_meta:
            pl.when(first == 0)(lambda: diag(LANES))
            pl.when(first > 0)(lambda: diag(0))
        else:
            diag(0)

        if online:
            inv = jnp.where(lo, 1.0 / l_s[0], 1.0 / l_s[1])
        else:
            inv = jnp.where(lo, 1.0 / jnp.sum(lp_s[0], axis=1, keepdims=True),
                            1.0 / jnp.sum(lp_s[1], axis=1, keepdims=True))
        o_ref[0, rows, :] = (acc_s[...] * inv * _sigmoid(gate_ref[0, rows, :])).astype(BF16)

    def steps(online):
        if subs == 1:
            run(online, 0)
        else:
            def body(sub, carry):
                run(online, sub)
                return carry

            lax.fori_loop(0, subs, body, 0)

    @pl.when(flag_ref[0] == 1)
    def _():
        steps(False)

    @pl.when(flag_ref[0] != 1)
    def _():
        steps(True)


def _fox_attn(flag, first, q, k, v, gate, meta=None):
    b, l, d = v.shape
    t = _row_tile(l, ATTN_TILE)
    nt = l // t
    subs = ATTN_SUBTILES if nt % ATTN_SUBTILES == 0 else 1
    lp = LANES if t % LANES == 0 else t
    in_specs = [
        pl.BlockSpec(memory_space=pltpu.SMEM),
        pl.BlockSpec(memory_space=pltpu.SMEM),
        pl.BlockSpec((2, 1, subs * t, LANES), lambda bi, hp, i: (hp, bi, i, 0)),
        pl.BlockSpec((2, 1, l, LANES), lambda bi, hp, i: (hp, bi, 0, 0)),
        pl.BlockSpec((1, l, LANES), lambda bi, hp, i: (bi, 0, hp)),
        pl.BlockSpec((1, subs * t, LANES), lambda bi, hp, i: (bi, i, hp)),
    ]
    args = [flag, first, q, k, v, gate]
    if meta is not None:
        km, vm = meta
        in_specs += [
            pl.BlockSpec((2, LANES, LANES), lambda bi, hp, i: (hp, 0, 0)),
            pl.BlockSpec((LANES, LANES), lambda bi, hp, i: (0, hp)),
        ]
        args += [km, vm]
    return pl.pallas_call(
        functools.partial(_fox_attn_kernel, meta is not None, t, nt, lp, subs),
        grid=(b, FOX_PAIRS, nt // subs),
        in_specs=in_specs,
        out_specs=pl.BlockSpec((1, subs * t, LANES), lambda bi, hp, i: (bi, i, hp)),
        out_shape=jax.ShapeDtypeStruct((b, l, d), BF16),
        scratch_shapes=[
            pltpu.VMEM((2, t, 1), F32),
            pltpu.VMEM((2, t, 1), F32),
            pltpu.VMEM((2, t, lp), F32),
            pltpu.VMEM((t, LANES), F32),
        ],
        compiler_params=_params("arbitrary", "arbitrary", "arbitrary"),
        name="fox_attn",
    )(*args)


def _ffn_kernel(final, nh, y_ref, wmix_ref, h_ref, gain_ref, win_ref, wout_ref, fgain_ref, o_ref,
                xn_s, acc_s):
    hm = h_ref[...] + _dot(y_ref[...], wmix_ref[...])
    xn_s[...] = _rms_rows(hm, gain_ref[...]).astype(BF16)
    acc_s[...] = hm

    for j in range(nh):
        xn = xn_s[...]
        g = _dot(xn, win_ref[:, j * FFN_TILE:(j + 1) * FFN_TILE])
        u = _dot(xn, win_ref[:, FFN_HIDDEN + j * FFN_TILE:FFN_HIDDEN + (j + 1) * FFN_TILE])
        acc_s[...] += _dot((g * _sigmoid(g) * u).astype(BF16), wout_ref[j * FFN_TILE:(j + 1) * FFN_TILE, :])

    out = acc_s[...]
    if final:
        out = _rms_rows(out, fgain_ref[...])
    o_ref[...] = out


def _ffn(y, w_mix, h, gain, w_in, w_out, fgain, final):
    m, d = h.shape
    tm = _row_tile(m, ROW_TILE)
    nh = FFN_HIDDEN // FFN_TILE
    row = lambda i: (i, 0)
    return pl.pallas_call(
        functools.partial(_ffn_kernel, final, nh),
        grid=(m // tm,),
        in_specs=[
            pl.BlockSpec((tm, d), row),
            _resident((d, d)),
            pl.BlockSpec((tm, d), row),
            _resident((1, d)),
            _resident((d, 2 * FFN_HIDDEN)),
            _resident((FFN_HIDDEN, d)),
            _resident((1, d)),
        ],
        out_specs=pl.BlockSpec((tm, d), row),
        out_shape=jax.ShapeDtypeStruct((m, d), F32),
        scratch_shapes=[pltpu.VMEM((tm, d), BF16), pltpu.VMEM((tm, d), F32)],
        compiler_params=_params("arbitrary"),
        name="ffn",
    )(y, w_mix, h, gain, w_in, w_out, fgain)


def _hgrn_proj_kernel(x_ref, gain_ref, w_ref, lbp_ref, q_out, lf_out, k_out, v_out, g_out):
    d = D_MODEL

    lbp = lbp_ref[...]
    mx = jnp.maximum(lbp[0:1], lbp[1:2])
    e0 = jnp.exp(lbp[0:1] - mx)
    e1 = jnp.exp(lbp[1:2] - mx)
    den = e0 + e1
    sm0 = e0 / den
    lb = (sm0 + e1 / den) - sm0

    tm = x_ref.shape[0]
    hr = tm // 2 if tm % 32 == 0 else tm
    for r0 in range(0, tm, hr):
        rs = slice(r0, r0 + hr)
        xh = _rms_rows(x_ref[rs, :], gain_ref[...]).astype(BF16)
        qz = _dot(xh, w_ref[:, 0:d])
        q_out[rs, :] = (qz * _sigmoid(qz)).astype(BF16)
        sig = _sigmoid(_dot(xh, w_ref[:, d:2 * d]))
        lf_out[rs, :] = jnp.log2(lb + (1.0 - lb) * sig)
        k_out[rs, :] = (1.0 - lb) * (1.0 - sig)
        v_out[rs, :] = _dot(xh, w_ref[:, 2 * d:3 * d]).astype(BF16)
        gz = _dot(xh, w_ref[:, 3 * d:4 * d])
        g_out[rs, :] = (gz * _sigmoid(gz)).astype(BF16)


def _hgrn_proj(x, gain, w, lbp):
    m, d = x.shape
    tm = _row_tile(m, ROW_TILE)
    row = lambda i: (i, 0)
    return pl.pallas_call(
        _hgrn_proj_kernel,
        grid=(m // tm,),
        in_specs=[pl.BlockSpec((tm, d), row), _resident((1, d)), _resident((d, 4 * d)), _resident((2, d))],
        out_specs=[pl.BlockSpec((tm, d), row)] * 5,
        out_shape=[jax.ShapeDtypeStruct((m, d), BF16), jax.ShapeDtypeStruct((m, d), F32),
                   jax.ShapeDtypeStruct((m, d), F32), jax.ShapeDtypeStruct((m, d), BF16),
                   jax.ShapeDtypeStruct((m, d), BF16)],
        compiler_params=_params("arbitrary"),
        name="hgrn_proj",
    )(x, gain, w, lbp)


def _hgrn_scan_kernel(t, c, q_ref, lf_ref, k_ref, i_ref, g_ref, gg_ref, tri_ref, s0_ref,
                      y_ref, sfin_ref, st_s, oi_s, oa_s, qin_s, kout_s, qf_s, kf_s, b_s, q_s, k_s, v_s):
    step = pl.program_id(1)
    nc = t // c
    mid = c // 2
    heads = [slice(h * HGRN_DK, (h + 1) * HGRN_DK) for h in range(HGRN_HEADS)]

    @pl.when(step == 0)
    def _():
        st_s[...] = s0_ref[...]

    qq = q_ref[0].astype(F32)
    kk = k_ref[0]
    v_b = i_ref[0]

    b = _tri_cumsum(tri_ref[...], lf_ref[0], 2)

    spread = jnp.float32(0.0)
    decay = []
    for ci in range(nc):
        rs = slice(ci * c, (ci + 1) * c)
        bc = b[rs]
        b_last = bc[c - 1:c]
        b_mid = bc[mid - 1:mid]
        spread = jnp.maximum(spread, jnp.maximum(jnp.max(bc[0:1] - b_mid), jnp.max(b_mid - b_last)))
        q_in = qq[rs] * jnp.exp2(bc)
        k_out = kk[rs] * jnp.exp2(b_last - bc)
        qin_s[rs] = q_in.astype(BF16)
        kout_s[rs] = k_out.astype(BF16)
        qf_s[rs] = (q_in * jnp.exp2(-b_mid)).astype(BF16)
        kf_s[rs] = (k_out * jnp.exp2(b_mid - b_last)).astype(BF16)
        decay.append(jnp.exp2(b_last))

    keep = tri_ref[...] > 0
    for sl in heads:
        a = jnp.where(keep, _dot(qf_s[:, sl], kf_s[:, sl], NT), 0.0)
        oa_s[:, sl] = _dot(a.astype(BF16), v_b[:, sl])

    for ci in range(nc):
        rs = slice(ci * c, (ci + 1) * c)
        for h, sl in enumerate(heads):
            st = st_s[h]
            oi_s[rs, sl] = _dot(qin_s[rs, sl], st.astype(BF16), NT)
            st_s[h] = st * decay[ci][:, sl] + _dot(v_b[rs, sl], kout_s[rs, sl], TN)

    @pl.when(spread > FAST_RANGE)
    def _():
        b_s[...] = b
        k_s[...] = kk
        q_s[...] = qq
        v_s[...] = v_b.astype(F32)
        oa_s[...] = jnp.zeros(oa_s.shape, F32)
        rowc = lax.broadcasted_iota(jnp.int32, (c, 1), 0)

        def body(s, carry):
            base = pl.multiple_of((s // c) * c, c)
            blk = pl.ds(base, c)
            p = (q_s[blk, :] * jnp.exp2(jnp.minimum(b_s[blk, :] - b_s[pl.ds(s, 1), :], 0.0))
                 * k_s[pl.ds(s, 1), :])
            vs = v_s[pl.ds(s, 1), :]
            seen = rowc + base >= s
            for sl in heads:
                w = jnp.where(seen, jnp.sum(p[:, sl], axis=1, keepdims=True), 0.0)
                oa_s[blk, sl] += w * vs[:, sl]
            return carry

        lax.fori_loop(0, t, body, 0)

    gate = g_ref[0].astype(F32)
    o = oi_s[...] + oa_s[...]
    for sl in heads:
        y_ref[0, :, sl] = (_rms_rows(o[:, sl], gg_ref[:, sl]) * gate[:, sl]).astype(BF16)

    @pl.when(step == pl.num_programs(1) - 1)
    def _():
        sfin_ref[...] = st_s[...]


def _hgrn_scan(q, lf, k, i, g, gg, s0):
    b, l, d = q.shape
    t = _row_tile(l, SCAN_TILE)
    c = _row_tile(t, HGRN_CHUNK)
    blk = lambda bi, s: (bi, s, 0)
    st_shape = (HGRN_HEADS, HGRN_DK, HGRN_DK)
    return pl.pallas_call(
        functools.partial(_hgrn_scan_kernel, t, c),
        grid=(b, l // t),
        in_specs=[
            pl.BlockSpec((1, t, d), blk),
            pl.BlockSpec((1, t, d), blk),
            pl.BlockSpec((1, t, d), blk),
            pl.BlockSpec((1, t, d), blk),
            pl.BlockSpec((1, t, d), blk),
            _resident((1, d)),
            _resident((t, t)),
            _resident(st_shape),
        ],
        out_specs=[pl.BlockSpec((1, t, d), blk), pl.BlockSpec(st_shape, lambda bi, s: (0, 0, 0))],
        out_shape=[jax.ShapeDtypeStruct((b, l, d), BF16), jax.ShapeDtypeStruct(st_shape, F32)],
        scratch_shapes=([pltpu.VMEM(st_shape, F32)] + [pltpu.VMEM((t, d), F32)] * 2
                        + [pltpu.VMEM((t, d), BF16)] * 4 + [pltpu.VMEM((t, d), F32)] * 4),
        compiler_params=_params("arbitrary", "arbitrary"),
        name="hgrn_scan",
    )(q, lf, k, i, g, gg, _block_tril(t, c), s0)


def kernel(x, meta_tokens, attn_norm, ffn_norm, final_norm, fox_w_in, fox_b_f, fox_q_norm, fox_k_norm,
           fox_w_out, hgrn_w_in, hgrn_lower_bounds, hgrn_g_norm, hgrn_w_out, ffn_w_in, ffn_w_out):
    bsz, seq, d = x.shape
    row = lambda v: v.reshape(1, -1).astype(F32)

    order = jnp.argsort(fox_b_f[0])
    cols = (order[:, None] * FOX_HEAD_DIM + jnp.arange(FOX_HEAD_DIM)[None, :]).reshape(-1)
    fox_b_sorted = fox_b_f[0][order]

    w_fox = fox_w_in[0, :, :4 * d].reshape(d, 4, d)[:, :, cols].reshape(d, 4 * d).astype(BF16)
    w_f = jnp.pad(fox_w_in[0, :, 4 * d:][:, order], ((0, 0), (0, LANES - FOX_HEADS))).astype(BF16)
    w_fox_out = fox_w_out[0][cols, :].astype(BF16)
    w_hgrn = hgrn_w_in[0].astype(BF16)
    w_hgrn_out = hgrn_w_out[0].astype(BF16)
    w_ffn_in = ffn_w_in.astype(BF16)
    w_ffn_out = ffn_w_out.astype(BF16)
    head_of = np.arange(d) // FOX_HEAD_DIM
    bd = jnp.asarray(head_of[:, None] == np.arange(LANES)[None, :], BF16)
    ex = jnp.concatenate([bd.T, bd.T], axis=0)
    qg = row(jnp.tile(fox_q_norm[0], FOX_HEADS)) * (FOX_HEAD_DIM ** -0.5 * LOG2E)
    kg = row(jnp.tile(fox_k_norm[0], FOX_HEADS))
    bound2 = (1.01 * LOG2E * FOX_HEAD_DIM ** 0.5 * jnp.max(jnp.abs(fox_q_norm[0]))
              * jnp.max(jnp.abs(fox_k_norm[0]))).astype(BF16).astype(F32)
    flag = (2.0 * bound2 <= GUARD_BITS).astype(jnp.int32).reshape(1)
    shifted = (bound2 + jnp.maximum(GUARD_BITS - 2.0 * bound2, 0.0)).astype(BF16).astype(F32)
    pq, pk = _placement(shifted)
    b_f = jnp.pad(row(fox_b_sorted), ((0, 0), (0, LANES - FOX_HEADS)))
    gg = row(jnp.tile(hgrn_g_norm[0], HGRN_HEADS))
    lbp = hgrn_lower_bounds.astype(F32)

    h_meta = meta_tokens.astype(F32)
    h_real = x.reshape(bsz * seq, d)
    seq3 = lambda a: a.reshape(bsz, seq, -1)

    proj = lambda h, n, c0: _fox_proj(h, n, row(attn_norm[0]), w_fox, w_f, b_f, c0, bd, ex, qg, kg, pq, pk)
    qm, km, vm, gm, c_meta = proj(h_meta, N_META, jnp.zeros((1, LANES), F32))
    qr, kr, vr, gr, c_tiles = proj(h_real, seq, c_meta[0])
    ym = _fox_attn(flag, jnp.zeros((FOX_PAIRS,), jnp.int32), qm[:, None], km[:, None], vm[None], gm[None])[0]
    pad_bias = np.zeros((FOX_HEADS, LANES, LANES), np.float32)
    for h in range(FOX_HEADS):
        pad_bias[h, N_META:, _bound_lane(h)] = PAD_KEY_BIAS
    km_pad = jnp.pad(km, ((0, 0), (0, LANES - N_META), (0, 0))) + jnp.asarray(pad_bias, BF16)
    vm_pad = jnp.pad(vm, ((0, LANES - N_META), (0, 0)))
    heads4 = lambda a: a.reshape(FOX_HEADS, bsz, seq, LANES)
    yr = _fox_attn(flag, _first_key_tile(c_tiles, c_meta[0], bsz, shifted - bound2), heads4(qr), heads4(kr), seq3(vr), seq3(gr),
                   meta=(km_pad, vm_pad)).reshape(bsz * seq, d)
    ffn0 = lambda y, h: _ffn(y, w_fox_out, h, row(ffn_norm[0]), w_ffn_in[0], w_ffn_out[0],
                             row(final_norm), False)
    h_meta = ffn0(ym, h_meta)
    h_real = ffn0(yr, h_real)

    proj = lambda h: _hgrn_proj(h, row(attn_norm[1]), w_hgrn, lbp)
    s0 = jnp.zeros((HGRN_HEADS, HGRN_DK, HGRN_DK), F32)
    _, s_meta = _hgrn_scan(*(a[None] for a in proj(h_meta)), gg, s0)
    yr, _ = _hgrn_scan(*(seq3(a) for a in proj(h_real)), gg, s_meta)
    out = _ffn(yr.reshape(bsz * seq, d), w_hgrn_out, h_real, row(ffn_norm[1]), w_ffn_in[1], w_ffn_out[1],
               row(final_norm), True)
    return out.reshape(bsz, seq, d)
```

```python
import functools
import math

import numpy as np
import jax
import jax.numpy as jnp
from jax import lax
from jax.experimental import pallas as pl
from jax.experimental.pallas import tpu as pltpu

D_MODEL = 1024
N_META = 16
FOX_HEADS = 16
FOX_HEAD_DIM = 64
HGRN_HEADS = 8
HGRN_DK = 128
HGRN_CHUNK = 64
FFN_HIDDEN = 2816
EPS = 1e-6

LANES = 128
FOX_PAIRS = FOX_HEADS // 2
FFN_TILE = 256
NEG_BIG = -1e30
PAD_KEY_BIAS = -30000.0
LOG2E = math.log2(math.e)
GUARD_BITS = 73.0
SKIP_BITS = 152.0
FAST_RANGE = 56.0
VMEM_LIMIT = 52 * 1024 * 1024

ROW_TILE = 512
ATTN_TILE = 512
ATTN_SUBTILES = 16
SCAN_TILE = 256

AUG_C = 0
AUG_ONE = 3
AUG_BOUND = 6
C3_MID, C3_LO, C3_ONE = 16, 32, 48

F32 = jnp.float32
BF16 = jnp.bfloat16

NN = (((1,), (0,)), ((), ()))
NT = (((1,), (1,)), ((), ()))
TN = (((0,), (0,)), ((), ()))


def _dot(a, b, dims=NN):
    return lax.dot_general(a, b, dims, preferred_element_type=F32)


def _params(*sem):
    return pltpu.CompilerParams(dimension_semantics=sem, vmem_limit_bytes=VMEM_LIMIT)


def _row_tile(m, pref):
    return pref if m % pref == 0 else m


def _resident(shape):
    return pl.BlockSpec(shape, lambda *_: (0,) * len(shape), pipeline_mode=pl.Buffered(1))


def _sigmoid(x):
    return 1.0 / (1.0 + jnp.exp(-x))


def _rms_rows(x, gain):
    ms = jnp.mean(x * x, axis=-1, keepdims=True)
    return x * lax.rsqrt(ms + EPS) * gain


def _split3(x):
    hi = x.astype(BF16)
    r1 = x - hi.astype(F32)
    mid = r1.astype(BF16)
    return hi, mid, (r1 - mid.astype(F32)).astype(BF16)


def _tri_cumsum(tri, x, pieces):
    if pieces == 3:
        hi, mid, lo = _split3(x)
        return _dot(tri, hi) + _dot(tri, mid) + _dot(tri, lo)
    hi = x.astype(BF16)
    return _dot(tri, hi) + _dot(tri, (x - hi.astype(F32)).astype(BF16))


def _block_tril(t, c):
    r = np.arange(t)
    return jnp.asarray((r[None, :] <= r[:, None]) & (r[None, :] // c == r[:, None] // c), BF16)


def _fox_proj_kernel(tiles_per_seq, x_ref, gain_ref, w_ref, wf_ref, bias_ref, c0_ref, tri_ref, bd_ref, ex_ref,
                     qg_ref, kg_ref, pq_ref, pk_ref, q_out, k_out, v_out, g_out, c_end, carry_s):
    @pl.when(lax.rem(pl.program_id(0), tiles_per_seq) == 0)
    def _():
        carry_s[...] = c0_ref[...]

    xn = _rms_rows(x_ref[...], gain_ref[...]).astype(BF16)
    lane = lax.broadcasted_iota(jnp.int32, (1, LANES), 1)
    lo = lane < FOX_HEAD_DIM

    zz = _dot(xn, wf_ref[...]) + bias_ref[...]
    log_f = jnp.minimum(zz, 0.0) - jnp.log1p(jnp.exp(-jnp.abs(zz)))
    c = _tri_cumsum(tri_ref[...], log_f, 3) + carry_s[...]
    tm = c.shape[0]
    carry_s[...] = c[tm - 1:tm, :]
    c_end[0] = c[tm - 1:tm, :]

    cc = jnp.where(lane < FOX_HEADS, c * LOG2E, 0.0)
    hi, mid, low = _split3(cc)
    c3 = (hi.astype(F32) + pltpu.roll(mid.astype(F32), C3_MID, axis=1)
          + pltpu.roll(low.astype(F32), C3_LO, axis=1) + jnp.where(lane == C3_ONE, 1.0, 0.0)).astype(BF16)

    def head_slabs(acc, gain, place_ref, out):
        ss = _dot((acc * acc).astype(BF16), bd_ref[...])
        r = lax.rsqrt(ss * (1.0 / FOX_HEAD_DIM) + EPS)
        r_hi = r.astype(BF16)
        r_lo = (r - r_hi.astype(F32)).astype(BF16)
        xh = acc * _dot(jnp.concatenate([r_hi, r_lo], axis=1), ex_ref[...]) * gain
        aug = _dot(c3, place_ref[...])
        for p in range(FOX_PAIRS):
            pair = slice(p * LANES, (p + 1) * LANES)
            out[2 * p] = jnp.where(lo, xh[:, pair], aug[:, pair]).astype(BF16)
            out[2 * p + 1] = jnp.where(lo, aug[:, pair], xh[:, pair]).astype(BF16)

    d = D_MODEL
    head_slabs(_dot(xn, w_ref[:, 0:d]), qg_ref[...], pq_ref, q_out)
    head_slabs(_dot(xn, w_ref[:, d:2 * d]), kg_ref[...], pk_ref, k_out)
    v_out[...] = _dot(xn, w_ref[:, 2 * d:3 * d]).astype(BF16)
    g_out[...] = _sigmoid(_dot(xn, w_ref[:, 3 * d:4 * d])).astype(BF16)


def _fox_proj(x, seq, gain, w, wf, bias, c0, bd, ex, qg, kg, pq, pk):
    m = x.shape[0]
    tm = _row_tile(seq, ROW_TILE)
    d = D_MODEL
    row = lambda i: (i, 0)
    slab = pl.BlockSpec((FOX_HEADS, tm, LANES), lambda i: (0, i, 0))
    return pl.pallas_call(
        functools.partial(_fox_proj_kernel, seq // tm),
        grid=(m // tm,),
        in_specs=[
            pl.BlockSpec((tm, d), row),
            _resident((1, d)),
            _resident((d, 4 * d)),
            _resident((d, LANES)),
            _resident((1, LANES)),
            _resident((1, LANES)),
            _resident((tm, tm)),
            _resident((d, LANES)),
            _resident((2 * LANES, d)),
            _resident((1, d)),
            _resident((1, d)),
            _resident((LANES, d)),
            _resident((LANES, d)),
        ],
        out_specs=[slab, slab, pl.BlockSpec((tm, d), row), pl.BlockSpec((tm, d), row),
                   pl.BlockSpec((1, 1, LANES), lambda i: (i, 0, 0))],
        out_shape=[
            jax.ShapeDtypeStruct((FOX_HEADS, m, LANES), BF16),
            jax.ShapeDtypeStruct((FOX_HEADS, m, LANES), BF16),
            jax.ShapeDtypeStruct((m, d), BF16),
            jax.ShapeDtypeStruct((m, d), BF16),
            jax.ShapeDtypeStruct((m // tm, 1, LANES), F32),
        ],
        scratch_shapes=[pltpu.VMEM((1, LANES), F32)],
        compiler_params=_params("arbitrary"),
        name="fox_proj",
    )(x, gain, w, wf, bias, c0, _block_tril(tm, tm), bd, ex, qg, kg, pq, pk)


def _bound_lane(h):
    return (FOX_HEAD_DIM if h % 2 == 0 else 0) + AUG_BOUND


def _placement(bound2):
    pq = np.zeros((LANES, D_MODEL), np.float32)
    pk = np.zeros((LANES, D_MODEL), np.float32)
    pb = np.zeros((LANES, D_MODEL), np.float32)
    for h in range(FOX_HEADS):
        base = (h // 2) * LANES + (FOX_HEAD_DIM if h % 2 == 0 else 0)
        for piece, src in enumerate((h, C3_MID + h, C3_LO + h)):
            pq[src, base + AUG_C + piece] = 1.0
            pk[src, base + AUG_ONE + piece] = -1.0
            pq[C3_ONE, base + AUG_ONE + piece] = 1.0
            pk[C3_ONE, base + AUG_C + piece] = 1.0
        pq[C3_ONE, base + AUG_BOUND] = 1.0
        pb[C3_ONE, base + AUG_BOUND] = 1.0
    return jnp.asarray(pq, BF16), (jnp.asarray(pk) - jnp.asarray(pb) * bound2).astype(BF16)


def _first_key_tile(c_tile_end, c_start, bsz, shift):
    nt = c_tile_end.shape[0] // bsz
    ce = c_tile_end.reshape(bsz, nt, LANES)[:, :, :FOX_HEADS] * LOG2E
    before = jnp.concatenate([jnp.broadcast_to(c_start[:, :FOX_HEADS] * LOG2E, (bsz, 1, FOX_HEADS)),
                              ce[:, :-1]], axis=1)
    gap = before[:, :, None, :] - ce[:, None, :, :]
    earlier = (jnp.arange(nt)[None, :] < jnp.arange(nt)[:, None])[None, :, :, None]
    dead = jnp.logical_and(gap - shift < -SKIP_BITS, earlier)
    dead = jnp.logical_and(dead[..., 0::2], dead[..., 1::2])
    first = jnp.sum(dead.astype(jnp.int32), axis=2)
    return first.transpose(0, 2, 1).reshape(-1)


def _fox_attn_kernel(has_meta, t, nt, lp, subs, flag_ref, first_ref, q_ref, k_ref, v_ref, gate_ref, *rest):
    if has_meta:
        km_ref, vm_ref, o_ref, m_s, l_s, lp_s, acc_s = rest
    else:
        o_ref, m_s, l_s, lp_s, acc_s = rest

    lane = lax.broadcasted_iota(jnp.int32, (1, LANES), 1)
    lo = lane < FOX_HEAD_DIM
    half = (lo, jnp.logical_not(lo))

    def run(online, sub):
        i = pl.program_id(2) * subs + sub
        rows = pl.ds(pl.multiple_of(sub * t, t), t)
        first = first_ref[(pl.program_id(0) * FOX_PAIRS + pl.program_id(1)) * nt + i]

        def tile(keys, vt, mask):
            alphas, pv = [], None
            zero_v = jnp.zeros_like(vt)
            for a in range(2):
                s = _dot(q_ref[a, 0, rows, :], keys(a), NT)
                if mask is not None:
                    s = jnp.where(mask, s, NEG_BIG)
                if online:
                    m_prev = m_s[a]
                    m_new = jnp.maximum(m_prev, jnp.max(s, axis=1, keepdims=True))
                    alpha = jnp.exp2(m_prev - m_new)
                    p = jnp.exp2(s - m_new)
                    l_s[a] = alpha * l_s[a] + jnp.sum(p, axis=1, keepdims=True)
                    m_s[a] = m_new
                    alphas.append(alpha)
                else:
                    p = jnp.exp2(s)
                    part = lp_s[a]
                    for c0 in range(0, p.shape[1], lp):
                        part = part + p[:, c0:c0 + lp]
                    lp_s[a] = part
                d = _dot(p.astype(BF16), jnp.where(half[a], vt, zero_v))
                pv = d if pv is None else pv + d
            if online:
                acc_s[...] = acc_s[...] * jnp.where(lo, alphas[0], alphas[1]) + pv
            else:
                acc_s[...] += pv

        def real_tile(off, size, mask):
            tile(lambda a: k_ref[a, 0, pl.ds(off, size), :], v_ref[0, pl.ds(off, size), :], mask)

        if online:
            m_s[...] = jnp.full(m_s.shape, NEG_BIG, F32)
            l_s[...] = jnp.zeros(l_s.shape, F32)
        else:
            lp_s[...] = jnp.zeros(lp_s.shape, F32)
        acc_s[...] = jnp.zeros(acc_s.shape, F32)

        if nt >= 2:
            lead = jnp.int32(0) if online else first
            n = i - lead

            @pl.when((n & 1) == 1)
            def _():
                real_tile(pl.multiple_of(lead * t, t), t, None)

            def body(j, carry):
                real_tile(pl.multiple_of((lead + (n & 1) + 2 * j) * t, t), 2 * t, None)
                return carry

            lax.fori_loop(0, lax.shift_right_logical(n, 1), body, 0)

        off = pl.multiple_of(i * t, t)

        def diag(n_pre):
            row = lax.broadcasted_iota(jnp.int32, (t, n_pre + t), 0)
            col = lax.broadcasted_iota(jnp.int32, (t, n_pre + t), 1)
            if n_pre:
                tile(lambda a: jnp.concatenate([km_ref[a], k_ref[a, 0, pl.ds(off, t), :]], axis=0),
                     jnp.concatenate([vm_ref[...], v_ref[0, pl.ds(off, t), :]], axis=0),
                     col - n_pre <= row)
            else:
                real_tile(off, t, col <= row)

        if has_meta and online:
            diag(LANES)
        elif has_meta:
            pl.when(first == 0)(lambda: diag(LANES))
            pl.when(first > 0)(lambda: diag(0))
        else:
            diag(0)

        if online:
            inv = jnp.where(lo, 1.0 / l_s[0], 1.0 / l_s[1])
        else:
            inv = jnp.where(lo, 1.0 / jnp.sum(lp_s[0], axis=1, keepdims=True),
                            1.0 / jnp.sum(lp_s[1], axis=1, keepdims=True))
        o_ref[0, rows, :] = (acc_s[...] * inv * gate_ref[0, rows, :].astype(F32)).astype(BF16)

    def steps(online):
        if subs == 1:
            run(online, 0)
        else:
            def body(sub, carry):
                run(online, sub)
                return carry

            lax.fori_loop(0, subs, body, 0)

    @pl.when(flag_ref[0] == 1)
    def _():
        steps(False)

    @pl.when(flag_ref[0] != 1)
    def _():
        steps(True)


def _fox_attn(flag, first, q, k, v, gate, meta=None):
    b, l, d = v.shape
    t = _row_tile(l, ATTN_TILE)
    nt = l // t
    subs = ATTN_SUBTILES if nt % ATTN_SUBTILES == 0 else 1
    lp = LANES if t % LANES == 0 else t
    in_specs = [
        pl.BlockSpec(memory_space=pltpu.SMEM),
        pl.BlockSpec(memory_space=pltpu.SMEM),
        pl.BlockSpec((2, 1, subs * t, LANES), lambda bi, hp, i: (hp, bi, i, 0)),
        pl.BlockSpec((2, 1, l, LANES), lambda bi, hp, i: (hp, bi, 0, 0)),
        pl.BlockSpec((1, l, LANES), lambda bi, hp, i: (bi, 0, hp)),
        pl.BlockSpec((1, subs * t, LANES), lambda bi, hp, i: (bi, i, hp)),
    ]
    args = [flag, first, q, k, v, gate]
    if meta is not None:
        km, vm = meta
        in_specs += [
            pl.BlockSpec((2, LANES, LANES), lambda bi, hp, i: (hp, 0, 0)),
            pl.BlockSpec((LANES, LANES), lambda bi, hp, i: (0, hp)),
        ]
        args += [km, vm]
    return pl.pallas_call(
        functools.partial(_fox_attn_kernel, meta is not None, t, nt, lp, subs),
        grid=(b, FOX_PAIRS, nt // subs),
        in_specs=in_specs,
        out_specs=pl.BlockSpec((1, subs * t, LANES), lambda bi, hp, i: (bi, i, hp)),
        out_shape=jax.ShapeDtypeStruct((b, l, d), BF16),
        scratch_shapes=[
            pltpu.VMEM((2, t, 1), F32),
            pltpu.VMEM((2, t, 1), F32),
            pltpu.VMEM((2, t, lp), F32),
            pltpu.VMEM((t, LANES), F32),
        ],
        compiler_params=_params("arbitrary", "arbitrary", "arbitrary"),
        name="fox_attn",
    )(*args)


def _ffn_kernel(final, nh, y_ref, wmix_ref, h_ref, gain_ref, win_ref, wout_ref, fgain_ref, o_ref,
                xn_s, acc_s):
    hm = h_ref[...] + _dot(y_ref[...], wmix_ref[...])
    xn_s[...] = _rms_rows(hm, gain_ref[...]).astype(BF16)
    acc_s[...] = hm

    for j in range(nh):
        xn = xn_s[...]
        g = _dot(xn, win_ref[:, j * FFN_TILE:(j + 1) * FFN_TILE])
        u = _dot(xn, win_ref[:, FFN_HIDDEN + j * FFN_TILE:FFN_HIDDEN + (j + 1) * FFN_TILE])
        acc_s[...] += _dot((g * _sigmoid(g) * u).astype(BF16), wout_ref[j * FFN_TILE:(j + 1) * FFN_TILE, :])

    out = acc_s[...]
    if final:
        out = _rms_rows(out, fgain_ref[...])
    o_ref[...] = out


def _ffn(y, w_mix, h, gain, w_in, w_out, fgain, final):
    m, d = h.shape
    tm = _row_tile(m, ROW_TILE)
    nh = FFN_HIDDEN // FFN_TILE
    row = lambda i: (i, 0)
    return pl.pallas_call(
        functools.partial(_ffn_kernel, final, nh),
        grid=(m // tm,),
        in_specs=[
            pl.BlockSpec((tm, d), row),
            _resident((d, d)),
            pl.BlockSpec((tm, d), row),
            _resident((1, d)),
            _resident((d, 2 * FFN_HIDDEN)),
            _resident((FFN_HIDDEN, d)),
            _resident((1, d)),
        ],
        out_specs=pl.BlockSpec((tm, d), row),
        out_shape=jax.ShapeDtypeStruct((m, d), F32),
        scratch_shapes=[pltpu.VMEM((tm, d), BF16), pltpu.VMEM((tm, d), F32)],
        compiler_params=_params("arbitrary"),
        name="ffn",
    )(y, w_mix, h, gain, w_in, w_out, fgain)


def _hgrn_proj_kernel(x_ref, gain_ref, w_ref, lbp_ref, q_out, lf_out, k_out, v_out, g_out):
    d = D_MODEL

    lbp = lbp_ref[...]
    mx = jnp.maximum(lbp[0:1], lbp[1:2])
    e0 = jnp.exp(lbp[0:1] - mx)
    e1 = jnp.exp(lbp[1:2] - mx)
    den = e0 + e1
    sm0 = e0 / den
    lb = (sm0 + e1 / den) - sm0

    tm = x_ref.shape[0]
    hr = tm // 2 if tm % 32 == 0 else tm
    for r0 in range(0, tm, hr):
        rs = slice(r0, r0 + hr)
        xh = _rms_rows(x_ref[rs, :], gain_ref[...]).astype(BF16)
        qz = _dot(xh, w_ref[:, 0:d])
        q_out[rs, :] = (qz * _sigmoid(qz)).astype(BF16)
        sig = _sigmoid(_dot(xh, w_ref[:, d:2 * d]))
        lf_out[rs, :] = jnp.log2(lb + (1.0 - lb) * sig)
        k_out[rs, :] = (1.0 - lb) * (1.0 - sig)
        v_out[rs, :] = _dot(xh, w_ref[:, 2 * d:3 * d]).astype(BF16)
        gz = _dot(xh, w_ref[:, 3 * d:4 * d])
        g_out[rs, :] = (gz * _sigmoid(gz)).astype(BF16)


def _hgrn_proj(x, gain, w, lbp):
    m, d = x.shape
    tm = _row_tile(m, ROW_TILE)
    row = lambda i: (i, 0)
    return pl.pallas_call(
        _hgrn_proj_kernel,
        grid=(m // tm,),
        in_specs=[pl.BlockSpec((tm, d), row), _resident((1, d)), _resident((d, 4 * d)), _resident((2, d))],
        out_specs=[pl.BlockSpec((tm, d), row)] * 5,
        out_shape=[jax.ShapeDtypeStruct((m, d), BF16), jax.ShapeDtypeStruct((m, d), F32),
                   jax.ShapeDtypeStruct((m, d), F32), jax.ShapeDtypeStruct((m, d), BF16),
                   jax.ShapeDtypeStruct((m, d), BF16)],
        compiler_params=_params("arbitrary"),
        name="hgrn_proj",
    )(x, gain, w, lbp)


def _hgrn_scan_kernel(t, c, q_ref, lf_ref, k_ref, i_ref, g_ref, gg_ref, tri_ref, s0_ref,
                      y_ref, sfin_ref, st_s, oi_s, oa_s, qin_s, kout_s, qf_s, kf_s, b_s, q_s, k_s, v_s):
    step = pl.program_id(1)
    nc = t // c
    mid = c // 2
    heads = [slice(h * HGRN_DK, (h + 1) * HGRN_DK) for h in range(HGRN_HEADS)]

    @pl.when(step == 0)
    def _():
        st_s[...] = s0_ref[...]

    qq = q_ref[0].astype(F32)
    kk = k_ref[0]
    v_b = i_ref[0]

    b = _tri_cumsum(tri_ref[...], lf_ref[0], 2)

    spread = jnp.float32(0.0)
    decay = []
    for ci in range(nc):
        rs = slice(ci * c, (ci + 1) * c)
        bc = b[rs]
        b_last = bc[c - 1:c]
        b_mid = bc[mid - 1:mid]
        spread = jnp.maximum(spread, jnp.maximum(jnp.max(bc[0:1] - b_mid), jnp.max(b_mid - b_last)))
        q_in = qq[rs] * jnp.exp2(bc)
        k_out = kk[rs] * jnp.exp2(b_last - bc)
        qin_s[rs] = q_in.astype(BF16)
        kout_s[rs] = k_out.astype(BF16)
        qf_s[rs] = (q_in * jnp.exp2(-b_mid)).astype(BF16)
        kf_s[rs] = (k_out * jnp.exp2(b_mid - b_last)).astype(BF16)
        decay.append(jnp.exp2(b_last))

    keep = tri_ref[...] > 0
    for sl in heads:
        a = jnp.where(keep, _dot(qf_s[:, sl], kf_s[:, sl], NT), 0.0)
        oa_s[:, sl] = _dot(a.astype(BF16), v_b[:, sl])

    for ci in range(nc):
        rs = slice(ci * c, (ci + 1) * c)
        for h, sl in enumerate(heads):
            st = st_s[h]
            oi_s[rs, sl] = _dot(qin_s[rs, sl], st.astype(BF16), NT)
            st_s[h] = st * decay[ci][:, sl] + _dot(v_b[rs, sl], kout_s[rs, sl], TN)

    @pl.when(spread > FAST_RANGE)
    def _():
        b_s[...] = b
        k_s[...] = kk
        q_s[...] = qq
        v_s[...] = v_b.astype(F32)
        oa_s[...] = jnp.zeros(oa_s.shape, F32)
        rowc = lax.broadcasted_iota(jnp.int32, (c, 1), 0)

        def body(s, carry):
            base = pl.multiple_of((s // c) * c, c)
            blk = pl.ds(base, c)
            p = (q_s[blk, :] * jnp.exp2(jnp.minimum(b_s[blk, :] - b_s[pl.ds(s, 1), :], 0.0))
                 * k_s[pl.ds(s, 1), :])
            vs = v_s[pl.ds(s, 1), :]
            seen = rowc + base >= s
            for sl in heads:
                w = jnp.where(seen, jnp.sum(p[:, sl], axis=1, keepdims=True), 0.0)
                oa_s[blk, sl] += w * vs[:, sl]
            return carry

        lax.fori_loop(0, t, body, 0)

    gate = g_ref[0].astype(F32)
    o = oi_s[...] + oa_s[...]
    for sl in heads:
        y_ref[0, :, sl] = (_rms_rows(o[:, sl], gg_ref[:, sl]) * gate[:, sl]).astype(BF16)

    @pl.when(step == pl.num_programs(1) - 1)
    def _():
        sfin_ref[...] = st_s[...]


def _hgrn_scan(q, lf, k, i, g, gg, s0):
    b, l, d = q.shape
    t = _row_tile(l, SCAN_TILE)
    c = _row_tile(t, HGRN_CHUNK)
    blk = lambda bi, s: (bi, s, 0)
    st_shape = (HGRN_HEADS, HGRN_DK, HGRN_DK)
    return pl.pallas_call(
        functools.partial(_hgrn_scan_kernel, t, c),
        grid=(b, l // t),
        in_specs=[
            pl.BlockSpec((1, t, d), blk),
            pl.BlockSpec((1, t, d), blk),
            pl.BlockSpec((1, t, d), blk),
            pl.BlockSpec((1, t, d), blk),
            pl.BlockSpec((1, t, d), blk),
            _resident((1, d)),
            _resident((t, t)),
            _resident(st_shape),
        ],
        out_specs=[pl.BlockSpec((1, t, d), blk), pl.BlockSpec(st_shape, lambda bi, s: (0, 0, 0))],
        out_shape=[jax.ShapeDtypeStruct((b, l, d), BF16), jax.ShapeDtypeStruct(st_shape, F32)],
        scratch_shapes=([pltpu.VMEM(st_shape, F32)] + [pltpu.VMEM((t, d), F32)] * 2
                        + [pltpu.VMEM((t, d), BF16)] * 4 + [pltpu.VMEM((t, d), F32)] * 4),
        compiler_params=_params("arbitrary", "arbitrary"),
        name="hgrn_scan",
    )(q, lf, k, i, g, gg, _block_tril(t, c), s0)


def kernel(x, meta_tokens, attn_norm, ffn_norm, final_norm, fox_w_in, fox_b_f, fox_q_norm, fox_k_norm,
           fox_w_out, hgrn_w_in, hgrn_lower_bounds, hgrn_g_norm, hgrn_w_out, ffn_w_in, ffn_w_out):
    bsz, seq, d = x.shape
    row = lambda v: v.reshape(1, -1).astype(F32)

    order = jnp.argsort(fox_b_f[0])
    cols = (order[:, None] * FOX_HEAD_DIM + jnp.arange(FOX_HEAD_DIM)[None, :]).reshape(-1)
    fox_b_sorted = fox_b_f[0][order]

    w_fox = fox_w_in[0, :, :4 * d].reshape(d, 4, d)[:, :, cols].reshape(d, 4 * d).astype(BF16)
    w_f = jnp.pad(fox_w_in[0, :, 4 * d:][:, order], ((0, 0), (0, LANES - FOX_HEADS))).astype(BF16)
    w_fox_out = fox_w_out[0][cols, :].astype(BF16)
    w_hgrn = hgrn_w_in[0].astype(BF16)
    w_hgrn_out = hgrn_w_out[0].astype(BF16)
    w_ffn_in = ffn_w_in.astype(BF16)
    w_ffn_out = ffn_w_out.astype(BF16)
    head_of = np.arange(d) // FOX_HEAD_DIM
    bd = jnp.asarray(head_of[:, None] == np.arange(LANES)[None, :], BF16)
    ex = jnp.concatenate([bd.T, bd.T], axis=0)
    qg = row(jnp.tile(fox_q_norm[0], FOX_HEADS)) * (FOX_HEAD_DIM ** -0.5 * LOG2E)
    kg = row(jnp.tile(fox_k_norm[0], FOX_HEADS))
    bound2 = (1.01 * LOG2E * FOX_HEAD_DIM ** 0.5 * jnp.max(jnp.abs(fox_q_norm[0]))
              * jnp.max(jnp.abs(fox_k_norm[0]))).astype(BF16).astype(F32)
    flag = (2.0 * bound2 <= GUARD_BITS).astype(jnp.int32).reshape(1)
    shifted = (bound2 + jnp.maximum(GUARD_BITS - 2.0 * bound2, 0.0)).astype(BF16).astype(F32)
    pq, pk = _placement(shifted)
    b_f = jnp.pad(row(fox_b_sorted), ((0, 0), (0, LANES - FOX_HEADS)))
    gg = row(jnp.tile(hgrn_g_norm[0], HGRN_HEADS))
    lbp = hgrn_lower_bounds.astype(F32)

    h_meta = meta_tokens.astype(F32)
    h_real = x.reshape(bsz * seq, d)
    seq3 = lambda a: a.reshape(bsz, seq, -1)

    proj = lambda h, n, c0: _fox_proj(h, n, row(attn_norm[0]), w_fox, w_f, b_f, c0, bd, ex, qg, kg, pq, pk)
    qm, km, vm, gm, c_meta = proj(h_meta, N_META, jnp.zeros((1, LANES), F32))
    qr, kr, vr, gr, c_tiles = proj(h_real, seq, c_meta[0])
    ym = _fox_attn(flag, jnp.zeros((FOX_PAIRS,), jnp.int32), qm[:, None], km[:, None], vm[None], gm[None])[0]
    pad_bias = np.zeros((FOX_HEADS, LANES, LANES), np.float32)
    for h in range(FOX_HEADS):
        pad_bias[h, N_META:, _bound_lane(h)] = PAD_KEY_BIAS
    km_pad = jnp.pad(km, ((0, 0), (0, LANES - N_META), (0, 0))) + jnp.asarray(pad_bias, BF16)
    vm_pad = jnp.pad(vm, ((0, LANES - N_META), (0, 0)))
    heads4 = lambda a: a.reshape(FOX_HEADS, bsz, seq, LANES)
    yr = _fox_attn(flag, _first_key_tile(c_tiles, c_meta[0], bsz, shifted - bound2), heads4(qr), heads4(kr), seq3(vr), seq3(gr),
                   meta=(km_pad, vm_pad)).reshape(bsz * seq, d)
    ffn0 = lambda y, h: _ffn(y, w_fox_out, h, row(ffn_norm[0]), w_ffn_in[0], w_ffn_out[0],
                             row(final_norm), False)
    h_meta = ffn0(ym, h_meta)
    h_real = ffn0(yr, h_real)

    proj = lambda h: _hgrn_proj(h, row(attn_norm[1]), w_hgrn, lbp)
    s0 = jnp.zeros((HGRN_HEADS, HGRN_DK, HGRN_DK), F32)
    _, s_meta = _hgrn_scan(*(a[None] for a in proj(h_meta)), gg, s0)
    yr, _ = _hgrn_scan(*(seq3(a) for a in proj(h_real)), gg, s_meta)
    out = _ffn(yr.reshape(bsz * seq, d), w_hgrn_out, h_real, row(ffn_norm[1]), w_ffn_in[1], w_ffn_out[1],
               row(final_norm), True)
    return out.reshape(bsz, seq, d)
```
